```python
import math
import jax
import jax.numpy as jnp
from jax import lax
import numpy as np

D_MODEL = 4096
BATCH = 1
SEQ = 8192
DEPTH = 1

HEAD_DIM = 128
MOBA_HEADS = 16
MOBA_WIDTH = MOBA_HEADS * HEAD_DIM
MOBA_BLOCK = 256
MOBA_TOPK = 3
MOBA_Q_CHUNK = 32
DIFF_HEADS = 8
DIFF_QK_WIDTH = DIFF_HEADS * 2 * HEAD_DIM
DIFF_V_DIM = 2 * HEAD_DIM
DIFF_V_WIDTH = DIFF_HEADS * DIFF_V_DIM
DIFF_Q_BLOCK = 128
N_BUCKETS = 32
MAX_DISTANCE = 128
N_BIAS_HEADS = MOBA_HEADS + DIFF_HEADS
IN_SPLITS = (MOBA_WIDTH, MOBA_WIDTH, MOBA_WIDTH, DIFF_QK_WIDTH, DIFF_QK_WIDTH, DIFF_V_WIDTH, D_MODEL, D_MODEL)
IN_WIDTH = sum(IN_SPLITS)
PEER_HEADS = 8
PEER_N_KEYS = 128
PEER_N_EXPERTS = PEER_N_KEYS * PEER_N_KEYS
PEER_HALF = 128
PEER_QUERY_DIM = 2 * PEER_HALF
PEER_TOPK = 16
PEER_TOKEN_CHUNK = 64
DEEPNORM_ALPHA = (2.0 * DEPTH) ** 0.25
DEEPNORM_BETA = (8.0 * DEPTH) ** -0.25
LN_EPS = 1e-5
NEG_INF = -1e30

kernel_name = 'hybrid_moba_diffattn_peer_block'


def layer_norm(x, g, b):
    xf = x.astype(jnp.float32)
    mu = jnp.mean(xf, axis=-1, keepdims=True)
    var = jnp.mean(jnp.square(xf - mu), axis=-1, keepdims=True)
    y = (xf - mu) * lax.rsqrt(var + LN_EPS) * g.astype(jnp.float32) + b.astype(jnp.float32)
    return y.astype(x.dtype)


def rms_norm(x, g):
    xf = x.astype(jnp.float32)
    y = xf * lax.rsqrt(jnp.mean(jnp.square(xf), axis=-1, keepdims=True) + LN_EPS) * g.astype(jnp.float32)
    return y.astype(x.dtype)


def t5_bucket(dist):
    n = jnp.maximum(dist, 0)
    max_exact = N_BUCKETS // 2
    nf = jnp.maximum(n, 1).astype(jnp.float32)
    large = max_exact + (jnp.log(nf / max_exact) / math.log(MAX_DISTANCE / max_exact)
                         * (N_BUCKETS - max_exact)).astype(jnp.int32)
    large = jnp.minimum(large, N_BUCKETS - 1)
    return jnp.where(n < max_exact, n, large)


def split_heads(t, n, d):
    b, s = t.shape[:2]
    return t.reshape(b, s, n, d).transpose(0, 2, 1, 3)


def merge_heads(o):
    b, h, s, d = o.shape
    return o.transpose(0, 2, 1, 3).reshape(b, s, h * d)


def seq_chunks(a, n_chunks):
    b, h, s = a.shape[:3]
    a = a.reshape(b, h, n_chunks, s // n_chunks, *a.shape[3:])
    return jnp.moveaxis(a, 2, 0)


def moba_attention(q, k, v, bias_tab):
    B, H, S, dh = q.shape
    nb = -(-S // MOBA_BLOCK)
    pad = nb * MOBA_BLOCK - S
    kp = jnp.pad(k, ((0, 0), (0, 0), (0, pad), (0, 0))).reshape(B, H, nb, MOBA_BLOCK, dh)
    vp = jnp.pad(v, ((0, 0), (0, 0), (0, pad), (0, 0))).reshape(B, H, nb, MOBA_BLOCK, dh)
    k_mean = jnp.mean(kp.astype(jnp.float32), axis=3)
    pos = jnp.arange(S, dtype=jnp.int32)
    q_blk = pos // MOBA_BLOCK
    gate = jnp.einsum('bhsd,bhnd->bhsn', q.astype(jnp.float32), k_mean)
    past = jnp.arange(nb, dtype=jnp.int32)[None, :] < q_blk[:, None]
    gate = jnp.where(past, gate, NEG_INF)
    _, sel = lax.top_k(gate, min(MOBA_TOPK, nb))
    sel = sel.astype(jnp.int32)
    sel_ok = sel < q_blk[:, None]
    own = jnp.broadcast_to(q_blk[None, None, :, None], (B, H, S, 1))
    blocks = jnp.concatenate([sel, own], axis=-1)
    valid = jnp.concatenate([sel_ok, jnp.ones((B, H, S, 1), dtype=bool)], axis=-1)
    n_chunks = S // MOBA_Q_CHUNK
    b_idx = jnp.arange(B)[:, None, None, None]
    h_idx = jnp.arange(H)[None, :, None, None]
    offs = jnp.arange(MOBA_BLOCK, dtype=jnp.int32)
    scale = HEAD_DIM ** -0.5

    def step(args):
        qc, blk, ok, qpos = args
        kg = kp[b_idx, h_idx, blk]
        vg = vp[b_idx, h_idx, blk]
        kpos = blk[..., None] * MOBA_BLOCK + offs
        dist = qpos[None, None, :, None, None] - kpos
        mask = ok[..., None] & (dist >= 0)
        bias = bias_tab[h_idx[..., None], t5_bucket(dist)].astype(jnp.float32)
        s = jnp.einsum('bhcd,bhcrld->bhcrl', qc, kg).astype(jnp.float32) * scale + bias
        s = jnp.where(mask, s, NEG_INF)
        p = jax.nn.softmax(s.reshape(s.shape[:3] + (-1,)), axis=-1).reshape(s.shape)
        return jnp.einsum('bhcrl,bhcrld->bhcd', p.astype(vg.dtype), vg)

    out = lax.map(step, (seq_chunks(q, n_chunks), seq_chunks(blocks, n_chunks),
                         seq_chunks(valid, n_chunks), pos.reshape(n_chunks, MOBA_Q_CHUNK)))
    return jnp.moveaxis(out, 0, 2).reshape(B, H, S, dh)


def diff_attention(q1, q2, k1, k2, v, lam, bias_tab):
    B, H, S, dh = q1.shape
    n_blocks = S // DIFF_Q_BLOCK
    kpos = jnp.arange(S, dtype=jnp.int32)
    scale = HEAD_DIM ** -0.5

    def step(args):
        a1, a2, qpos = args
        dist = qpos[:, None] - kpos[None, :]
        mask = dist >= 0
        bias = bias_tab[:, t5_bucket(dist)].astype(jnp.float32)

        def probs(qq, kk):
            s = jnp.einsum('bhqd,bhkd->bhqk', qq, kk).astype(jnp.float32) * scale + bias
            return jax.nn.softmax(jnp.where(mask, s, NEG_INF), axis=-1)

        a = probs(a1, k1) - lam * probs(a2, k2)
        return jnp.einsum('bhqk,bhkd->bhqd', a.astype(v.dtype), v)

    out = lax.map(step, (seq_chunks(q1, n_blocks), seq_chunks(q2, n_blocks),
                         kpos.reshape(n_blocks, DIFF_Q_BLOCK)))
    return jnp.moveaxis(out, 0, 2).reshape(B, H, S, 2 * dh)


def mixer_sublayer(x, w_in, gate_b, rel_bias, lq1, lk1, lq2, lk2, subln_g,
                   w_o_moba, w_o_diff, w_out, lambda_init):
    B, S, _ = x.shape
    proj = x @ w_in
    cuts = np.cumsum(IN_SPLITS)[:-1].tolist()
    mq, mk, mv, dq, dk, dv, ga, gb = jnp.split(proj, cuts, axis=-1)
    y_a = moba_attention(split_heads(mq, MOBA_HEADS, HEAD_DIM), split_heads(mk, MOBA_HEADS, HEAD_DIM),
                         split_heads(mv, MOBA_HEADS, HEAD_DIM), rel_bias[:, :MOBA_HEADS].T)
    dq = dq.reshape(B, S, DIFF_HEADS, 2, HEAD_DIM).transpose(0, 2, 3, 1, 4)
    dk = dk.reshape(B, S, DIFF_HEADS, 2, HEAD_DIM).transpose(0, 2, 3, 1, 4)
    lam = (jnp.exp(jnp.sum(lq1.astype(jnp.float32) * lk1.astype(jnp.float32)))
           - jnp.exp(jnp.sum(lq2.astype(jnp.float32) * lk2.astype(jnp.float32))) + lambda_init)
    y_b = diff_attention(dq[:, :, 0], dq[:, :, 1], dk[:, :, 0], dk[:, :, 1],
                         split_heads(dv, DIFF_HEADS, DIFF_V_DIM), lam, rel_bias[:, MOBA_HEADS:].T)
    y_b = rms_norm(y_b, subln_g) * (1.0 - lambda_init)
    g_a = jax.nn.sigmoid(ga + gate_b[:D_MODEL])
    g_b = jax.nn.sigmoid(gb + gate_b[D_MODEL:])
    merged = g_a * (merge_heads(y_a) @ w_o_moba) + g_b * (merge_heads(y_b) @ w_o_diff)
    return merged @ w_out


def peer_ffn(x, w_q, keys1, keys2, u_tab, v_tab):
    B, S, D = x.shape
    T = B * S
    xt = x.reshape(T, D)
    q = (xt @ w_q).reshape(T, PEER_HEADS, 2, PEER_HALF)
    s1 = jnp.einsum('thd,kd->thk', q[:, :, 0], keys1).astype(jnp.float32)
    s2 = jnp.einsum('thd,kd->thk', q[:, :, 1], keys2).astype(jnp.float32)
    v1, i1 = lax.top_k(s1, PEER_TOPK)
    v2, i2 = lax.top_k(s2, PEER_TOPK)
    cand = (v1[..., :, None] + v2[..., None, :]).reshape(T, PEER_HEADS, PEER_TOPK * PEER_TOPK)
    cidx = (i1[..., :, None] * PEER_N_KEYS + i2[..., None, :]).reshape(T, PEER_HEADS, PEER_TOPK * PEER_TOPK)
    top_s, top_pos = lax.top_k(cand, PEER_TOPK)
    eidx = jnp.take_along_axis(cidx, top_pos, axis=-1).reshape(T, PEER_HEADS * PEER_TOPK)
    gates = jax.nn.softmax(top_s, axis=-1).reshape(T, PEER_HEADS * PEER_TOPK)
    n_chunks = T // PEER_TOKEN_CHUNK

    def step(args):
        xc, ec, gc = args
        u = u_tab[ec]
        act = jax.nn.gelu(jnp.einsum('cd,ced->ce', xc, u).astype(jnp.float32), approximate=False)
        w = (gc * act).astype(v_tab.dtype)
        return jnp.einsum('ce,ced->cd', w, v_tab[ec])

    out = lax.map(step, (xt.reshape(n_chunks, PEER_TOKEN_CHUNK, D),
                         eidx.reshape(n_chunks, PEER_TOKEN_CHUNK, -1),
                         gates.reshape(n_chunks, PEER_TOKEN_CHUNK, -1)))
    return out.reshape(B, S, D).astype(x.dtype)


def setup_inputs(seed: int = 0) -> dict:
    key = jax.random.key(seed)
    ks = jax.random.split(key, 24)
    f32 = jnp.float32
    col_scale = jnp.concatenate([
        jnp.ones((2 * MOBA_WIDTH,), f32), jnp.full((MOBA_WIDTH,), DEEPNORM_BETA, f32),
        jnp.ones((2 * DIFF_QK_WIDTH,), f32), jnp.full((DIFF_V_WIDTH,), DEEPNORM_BETA, f32),
        jnp.ones((2 * D_MODEL,), f32)])
    d_is = D_MODEL ** -0.5
    return {
        'x': jax.random.normal(ks[0], (BATCH, SEQ, D_MODEL), f32),
        'w_in': jax.random.normal(ks[1], (DEPTH, D_MODEL, IN_WIDTH), f32) * d_is * col_scale,
        'gate_b': jax.random.normal(ks[2], (DEPTH, 2 * D_MODEL), f32) * 0.02,
        'rel_bias': jax.random.normal(ks[3], (N_BUCKETS, N_BIAS_HEADS), f32) * 0.5,
        'lambda_q1': jax.random.normal(ks[4], (DEPTH, HEAD_DIM), f32) * 0.1,
        'lambda_k1': jax.random.normal(ks[5], (DEPTH, HEAD_DIM), f32) * 0.1,
        'lambda_q2': jax.random.normal(ks[6], (DEPTH, HEAD_DIM), f32) * 0.1,
        'lambda_k2': jax.random.normal(ks[7], (DEPTH, HEAD_DIM), f32) * 0.1,
        'subln_g': 1.0 + 0.02 * jax.random.normal(ks[8], (DEPTH, DIFF_V_DIM), f32),
        'w_o_moba': jax.random.normal(ks[9], (DEPTH, MOBA_WIDTH, D_MODEL), f32) * (MOBA_WIDTH ** -0.5) * DEEPNORM_BETA,
        'w_o_diff': jax.random.normal(ks[10], (DEPTH, DIFF_V_WIDTH, D_MODEL), f32) * (DIFF_V_WIDTH ** -0.5) * DEEPNORM_BETA,
        'w_out': jax.random.normal(ks[11], (DEPTH, D_MODEL, D_MODEL), f32) * d_is * DEEPNORM_BETA,
        'ln1_g': 1.0 + 0.02 * jax.random.normal(ks[12], (DEPTH, D_MODEL), f32),
        'ln1_b': 0.02 * jax.random.normal(ks[13], (DEPTH, D_MODEL), f32),
        'peer_wq': jax.random.normal(ks[14], (DEPTH, D_MODEL, PEER_HEADS * PEER_QUERY_DIM), f32) * d_is,
        'peer_keys1': jax.random.normal(ks[15], (DEPTH, PEER_N_KEYS, PEER_HALF), f32) * (PEER_HALF ** -0.5),
        'peer_keys2': jax.random.normal(ks[16], (DEPTH, PEER_N_KEYS, PEER_HALF), f32) * (PEER_HALF ** -0.5),
        'peer_u': jax.random.normal(ks[17], (DEPTH, PEER_N_EXPERTS, D_MODEL), f32) * d_is,
        'peer_v': jax.random.normal(ks[18], (DEPTH, PEER_N_EXPERTS, D_MODEL), f32) * DEEPNORM_BETA,
        'ln2_g': 1.0 + 0.02 * jax.random.normal(ks[19], (DEPTH, D_MODEL), f32),
        'ln2_b': 0.02 * jax.random.normal(ks[20], (DEPTH, D_MODEL), f32),
    }


def reference(x, w_in, gate_b, rel_bias, lambda_q1, lambda_k1, lambda_q2, lambda_k2, subln_g,
              w_o_moba, w_o_diff, w_out, ln1_g, ln1_b, peer_wq, peer_keys1, peer_keys2,
              peer_u, peer_v, ln2_g, ln2_b):
    for layer in range(DEPTH):
        lambda_init = 0.8 - 0.6 * math.exp(-0.3 * layer)
        h = mixer_sublayer(x, w_in[layer], gate_b[layer], rel_bias, lambda_q1[layer], lambda_k1[layer],
                           lambda_q2[layer], lambda_k2[layer], subln_g[layer], w_o_moba[layer],
                           w_o_diff[layer], w_out[layer], lambda_init)
        x = layer_norm(DEEPNORM_ALPHA * x + h, ln1_g[layer], ln1_b[layer])
        h = peer_ffn(x, peer_wq[layer], peer_keys1[layer], peer_keys2[layer], peer_u[layer], peer_v[layer])
        x = layer_norm(DEEPNORM_ALPHA * x + h, ln2_g[layer], ln2_b[layer])
    return x
```

```python
import functools
import math

import jax
import jax.numpy as jnp
import numpy as np
from jax import lax
from jax.experimental import pallas as pl
from jax.experimental.pallas import tpu as pltpu

D_MODEL = 4096
SEQ = 8192
DEPTH = 1
HEAD_DIM = 128
MOBA_HEADS = 16
MOBA_WIDTH = MOBA_HEADS * HEAD_DIM
MOBA_BLOCK = 256
MOBA_TOPK = 3
N_MOBA_BLOCKS = SEQ // MOBA_BLOCK
DIFF_HEADS = 8
DIFF_V_DIM = 2 * HEAD_DIM
DIFF_WIDTH = DIFF_HEADS * DIFF_V_DIM
N_BUCKETS = 32
MAX_DISTANCE = 128
PEER_HEADS = 8
PEER_N_KEYS = 128
PEER_HALF = 128
PEER_QUERY_DIM = 2 * PEER_HALF
PEER_TOPK = 16
DEEPNORM_ALPHA = (2.0 * DEPTH) ** 0.25
LN_EPS = 1e-5
NEG_INF = -1e30

LOG2E = math.log2(math.e)
QK_SCALE_LOG2 = HEAD_DIM ** -0.5 * LOG2E

LANES = 128
VMEM_LIMIT_CAP = 60000 * 1024
COMPILER_SCRATCH_BYTES = 8 << 20

ATTN_TILE = 512


def _vmem_limit(*block_bytes):
    return min(2 * sum(block_bytes) + COMPILER_SCRATCH_BYTES, VMEM_LIMIT_CAP)


def _nbytes(shape, dtype):
    return int(np.prod(shape)) * jnp.dtype(dtype).itemsize


def _params(semantics, *block_bytes):
    return pltpu.CompilerParams(dimension_semantics=semantics,
                                vmem_limit_bytes=_vmem_limit(*block_bytes))


def _matmul_kernel(a_ref, b_ref, o_ref):
    o_ref[...] = jnp.dot(a_ref[...], b_ref[...],
                         preferred_element_type=jnp.float32).astype(o_ref.dtype)


def _matmul(a, b, out_dtype, tm, tn, name):
    m, k = a.shape
    n = b.shape[1]
    return pl.pallas_call(
        _matmul_kernel,
        grid=(n // tn, m // tm),
        in_specs=[pl.BlockSpec((tm, k), lambda j, i: (i, 0)),
                  pl.BlockSpec((k, tn), lambda j, i: (0, j))],
        out_specs=pl.BlockSpec((tm, tn), lambda j, i: (i, j)),
        out_shape=jax.ShapeDtypeStruct((m, n), out_dtype),
        compiler_params=_params(("arbitrary", "arbitrary"),
                                _nbytes((tm, k), a.dtype), _nbytes((k, tn), b.dtype),
                                _nbytes((tm, tn), out_dtype)),
        name=name,
    )(a, b)


def _moba_route_kernel(q_ref, k_ref, qa_ref, ka_ref, kmean_ref, *, tq):
    i = pl.program_id(1)

    @pl.when(i == 0)
    def _():
        k3 = k_ref[...].reshape(N_MOBA_BLOCKS, MOBA_BLOCK, HEAD_DIM)
        kmean_ref[...] = jnp.zeros_like(kmean_ref)
        kmean_ref[0:N_MOBA_BLOCKS, :] = jnp.sum(k3, axis=1) * (1.0 / MOBA_BLOCK)

    q = q_ref[...]
    gate = lax.dot_general(q, kmean_ref[...], (((1,), (1,)), ((), ())),
                           precision=lax.Precision.HIGHEST,
                           preferred_element_type=jnp.float32)
    lane = lax.broadcasted_iota(jnp.int32, (tq, LANES), 1)
    row = lax.broadcasted_iota(jnp.int32, (tq, LANES), 0)
    own = jnp.right_shift(i * tq + row, int(math.log2(MOBA_BLOCK)))
    lane_f = lane.astype(jnp.float32)
    past = lane < own
    g = jnp.where(past, gate, NEG_INF)
    picked = jnp.zeros((tq, LANES), jnp.float32)
    for _ in range(MOBA_TOPK):
        mx = jnp.max(g, axis=1, keepdims=True)
        first = jnp.min(jnp.where(g == mx, lane_f, float(LANES)), axis=1, keepdims=True)
        hit = lane_f == first
        picked = jnp.where(hit, 1.0, picked)
        g = jnp.where(hit, -jnp.inf, g)
    allowed = jnp.where(past, picked, 0.0)
    allowed = jnp.where(lane == own, 1.0, allowed)
    blocked = jnp.where(lane < N_MOBA_BLOCKS, 1.0 - allowed, 0.0)
    qa_ref[:, :HEAD_DIM] = (q * QK_SCALE_LOG2).astype(qa_ref.dtype)
    qa_ref[:, HEAD_DIM:] = blocked.astype(qa_ref.dtype)
    kb = k_ref[pl.ds(pl.multiple_of(i * tq, tq), tq), :]
    ka_ref[:, :HEAD_DIM] = kb.astype(ka_ref.dtype)
    ka_ref[:, HEAD_DIM:] = jnp.where(lane == own, NEG_INF, 0.0).astype(ka_ref.dtype)


def _moba_route(mqk, tq=512):
    s = mqk.shape[0]
    aug = 2 * HEAD_DIM
    out = jax.ShapeDtypeStruct((s, MOBA_HEADS * aug), jnp.bfloat16)
    return pl.pallas_call(
        functools.partial(_moba_route_kernel, tq=tq),
        grid=(MOBA_HEADS, s // tq),
        in_specs=[pl.BlockSpec((tq, HEAD_DIM), lambda h, i: (i, h)),
                  pl.BlockSpec((s, HEAD_DIM), lambda h, i: (0, MOBA_HEADS + h))],
        out_specs=[pl.BlockSpec((tq, aug), lambda h, i: (i, h)),
                   pl.BlockSpec((tq, aug), lambda h, i: (i, h))],
        out_shape=[out, out],
        scratch_shapes=[pltpu.VMEM((LANES, HEAD_DIM), jnp.float32)],
        compiler_params=_params(("arbitrary", "arbitrary"),
                                _nbytes((tq, HEAD_DIM), jnp.float32),
                                _nbytes((s, HEAD_DIM), jnp.float32),
                                2 * _nbytes((tq, aug), jnp.bfloat16)),
        name="moba_route",
    )(mqk, mqk)


def _t5_bucket_np(dist):
    n = np.maximum(dist, 0)
    max_exact = N_BUCKETS // 2
    nf = np.maximum(n, 1).astype(np.float32)
    large = max_exact + (np.log(nf / max_exact) / math.log(MAX_DISTANCE / max_exact)
                         * (N_BUCKETS - max_exact)).astype(np.int32)
    return np.minimum(large, N_BUCKETS - 1)


def _t5_bucket(dist):
    n = jnp.maximum(dist, 0)
    max_exact = N_BUCKETS // 2
    nf = jnp.maximum(n, 1).astype(jnp.float32)
    large = max_exact + (jnp.log(nf / max_exact) / math.log(MAX_DISTANCE / max_exact)
                         * (N_BUCKETS - max_exact)).astype(jnp.int32)
    large = jnp.minimum(large, N_BUCKETS - 1)
    return jnp.where(n < max_exact, n, large)


def _bias_tiles(rel_bias, t):
    assert int(_t5_bucket_np(np.array([t + 1]))[0]) == N_BUCKETS - 1
    tab = rel_bias.T.astype(jnp.float32)
    d0 = jnp.arange(t, dtype=jnp.int32)[:, None] - jnp.arange(t, dtype=jnp.int32)[None, :]
    far = tab[:, N_BUCKETS - 1][:, None, None]
    b0 = jnp.where(d0 >= 0, (tab[:, _t5_bucket(d0)] - far) * LOG2E, NEG_INF)
    b1 = (tab[:, _t5_bucket(d0 + t)] - far) * LOG2E
    return b0, b1


def _flash_branch(q, k_ref, v_ref, b0_ref, b1_ref, i, t):
    dv = v_ref.shape[-1]

    def step(j, carry, bias):
        m, l, acc = carry
        off = pl.multiple_of(j * t, t)
        k = k_ref[pl.ds(off, t), :]
        v = v_ref[pl.ds(off, t), :]
        s = lax.dot_general(q, k, (((1,), (1,)), ((), ())), preferred_element_type=jnp.float32)
        if bias is not None:
            s = s + bias
        m_new = jnp.maximum(m, jnp.max(s, axis=1, keepdims=True))
        alpha = jnp.exp2(m - m_new)
        p = jnp.exp2(s - m_new)
        l = alpha * l + jnp.sum(p, axis=1, keepdims=True)
        acc = alpha * acc + jnp.dot(p.astype(v.dtype), v, preferred_element_type=jnp.float32)
        return m_new, l, acc

    carry = (jnp.full((t, 1), -jnp.inf, jnp.float32), jnp.zeros((t, 1), jnp.float32),
             jnp.zeros((t, dv), jnp.float32))
    prev = jnp.maximum(i - 1, 0)
    carry = lax.fori_loop(0, prev, lambda j, c: step(j, c, None), carry)
    carry = step(prev, carry, b1_ref[...] + jnp.where(i == 0, NEG_INF, 0.0))
    m, l, acc = step(i, carry, b0_ref[...])
    return acc / l


def _moba_attn_kernel(q_ref, k_ref, v_ref, b0_ref, b1_ref, o_ref, *, t):
    o = _flash_branch(q_ref[...], k_ref, v_ref, b0_ref, b1_ref, pl.program_id(1), t)
    o_ref[...] = o.astype(o_ref.dtype)


def _moba_attention(qa, ka, v, v_col0, b0, b1, t=ATTN_TILE):
    s = qa.shape[0]
    aug = 2 * HEAD_DIM
    return pl.pallas_call(
        functools.partial(_moba_attn_kernel, t=t),
        grid=(MOBA_HEADS, s // t),
        in_specs=[pl.BlockSpec((t, aug), lambda h, i: (i, h)),
                  pl.BlockSpec((s, aug), lambda h, i: (0, h)),
                  pl.BlockSpec((s, HEAD_DIM), lambda h, i: (0, v_col0 + h)),
                  pl.BlockSpec((None, t, t), lambda h, i: (h, 0, 0)),
                  pl.BlockSpec((None, t, t), lambda h, i: (h, 0, 0))],
        out_specs=pl.BlockSpec((t, HEAD_DIM), lambda h, i: (i, h)),
        out_shape=jax.ShapeDtypeStruct((s, MOBA_WIDTH), jnp.bfloat16),
        compiler_params=_params(("arbitrary", "arbitrary"),
                                _nbytes((t, aug), jnp.bfloat16), _nbytes((s, aug), jnp.bfloat16),
                                _nbytes((s, HEAD_DIM), jnp.bfloat16),
                                2 * _nbytes((t, t), jnp.float32),
                                _nbytes((t, HEAD_DIM), jnp.bfloat16)),
        name="moba_attention",
    )(qa, ka, v, b0, b1)


def _diff_attn_kernel(lam_ref, q1_ref, q2_ref, k1_ref, k2_ref, v_ref, b0_ref, b1_ref, g_ref, o_ref,
                      *, t, out_scale):
    i = pl.program_id(1)
    o1 = _flash_branch(q1_ref[...], k1_ref, v_ref, b0_ref, b1_ref, i, t)
    o2 = _flash_branch(q2_ref[...], k2_ref, v_ref, b0_ref, b1_ref, i, t)
    y = o1 - lam_ref[0, 0] * o2
    y = y * lax.rsqrt(jnp.mean(y * y, axis=1, keepdims=True) + LN_EPS) * g_ref[...] * out_scale
    o_ref[...] = y.astype(o_ref.dtype)


def _diff_attention(lam, proj, q_col0, k_col0, v_col0, b0, b1, subln_g, out_scale, t=ATTN_TILE):
    s = proj.shape[0]
    return pl.pallas_call(
        functools.partial(_diff_attn_kernel, t=t, out_scale=out_scale),
        grid=(DIFF_HEADS, s // t),
        in_specs=[pl.BlockSpec(memory_space=pltpu.SMEM),
                  pl.BlockSpec((t, HEAD_DIM), lambda h, i: (i, q_col0 + 2 * h)),
                  pl.BlockSpec((t, HEAD_DIM), lambda h, i: (i, q_col0 + 2 * h + 1)),
                  pl.BlockSpec((s, HEAD_DIM), lambda h, i: (0, k_col0 + 2 * h)),
                  pl.BlockSpec((s, HEAD_DIM), lambda h, i: (0, k_col0 + 2 * h + 1)),
                  pl.BlockSpec((s, DIFF_V_DIM), lambda h, i: (0, v_col0 + h)),
                  pl.BlockSpec((None, t, t), lambda h, i: (MOBA_HEADS + h, 0, 0)),
                  pl.BlockSpec((None, t, t), lambda h, i: (MOBA_HEADS + h, 0, 0)),
                  pl.BlockSpec((1, DIFF_V_DIM), lambda h, i: (0, 0))],
        out_specs=pl.BlockSpec((t, DIFF_V_DIM), lambda h, i: (i, h)),
        out_shape=jax.ShapeDtypeStruct((s, DIFF_WIDTH), jnp.bfloat16),
        compiler_params=_params(("arbitrary", "arbitrary"),
                                2 * _nbytes((t, HEAD_DIM), jnp.bfloat16),
                                2 * _nbytes((s, HEAD_DIM), jnp.bfloat16),
                                _nbytes((s, DIFF_V_DIM), jnp.bfloat16),
                                2 * _nbytes((t, t), jnp.float32),
                                _nbytes((t, DIFF_V_DIM), jnp.bfloat16)),
        name="diff_attention",
    )(lam, proj, proj, proj, proj, proj, b0, b1, subln_g)


def _sigmoid(x):
    return 1.0 / (1.0 + jnp.exp(-x))


def _merge_kernel(ya_ref, yb_ref, wa_ref, wb_ref, ga_ref, gb_ref, ba_ref, bb_ref, o_ref):
    pa = jnp.dot(ya_ref[...], wa_ref[...], preferred_element_type=jnp.float32)
    pb = jnp.dot(yb_ref[...], wb_ref[...], preferred_element_type=jnp.float32)
    merged = (_sigmoid(ga_ref[...] + ba_ref[...]) * pa + _sigmoid(gb_ref[...] + bb_ref[...]) * pb)
    o_ref[...] = merged.astype(o_ref.dtype)


def _gated_merge(ya, yb, wa, wb, gates, gate_b, tm=512, tn=1024):
    s, k = ya.shape
    n = wa.shape[1]
    nb = n // tn
    return pl.pallas_call(
        _merge_kernel,
        grid=(nb, s // tm),
        in_specs=[pl.BlockSpec((tm, k), lambda j, i: (i, 0)),
                  pl.BlockSpec((tm, k), lambda j, i: (i, 0)),
                  pl.BlockSpec((k, tn), lambda j, i: (0, j)),
                  pl.BlockSpec((k, tn), lambda j, i: (0, j)),
                  pl.BlockSpec((tm, tn), lambda j, i: (i, j)),
                  pl.BlockSpec((tm, tn), lambda j, i: (i, nb + j)),
                  pl.BlockSpec((1, tn), lambda j, i: (0, j)),
                  pl.BlockSpec((1, tn), lambda j, i: (0, nb + j))],
        out_specs=pl.BlockSpec((tm, tn), lambda j, i: (i, j)),
        out_shape=jax.ShapeDtypeStruct((s, n), jnp.bfloat16),
        compiler_params=_params(("arbitrary", "arbitrary"),
                                2 * _nbytes((tm, k), ya.dtype), 2 * _nbytes((k, tn), wa.dtype),
                                2 * _nbytes((tm, tn), gates.dtype),
                                _nbytes((tm, tn), jnp.bfloat16)),
        name="gated_merge",
    )(ya, yb, wa, wb, gates, gates, gate_b, gate_b)


def _residual_ln_kernel(x_ref, h_ref, g_ref, b_ref, *o_refs):
    z = DEEPNORM_ALPHA * x_ref[...] + h_ref[...]
    mu = jnp.mean(z, axis=1, keepdims=True)
    zc = z - mu
    var = jnp.mean(zc * zc, axis=1, keepdims=True)
    y = zc * lax.rsqrt(var + LN_EPS) * g_ref[...] + b_ref[...]
    for o_ref in o_refs:
        o_ref[...] = y.astype(o_ref.dtype)


def _residual_ln(x, h, g, b, out_dtypes, tm=256):
    s, d = x.shape
    row = pl.BlockSpec((tm, d), lambda i: (i, 0))
    vec = pl.BlockSpec((1, d), lambda i: (0, 0))
    return pl.pallas_call(
        _residual_ln_kernel,
        grid=(s // tm,),
        in_specs=[row, row, vec, vec],
        out_specs=[row for _ in out_dtypes],
        out_shape=[jax.ShapeDtypeStruct((s, d), dt) for dt in out_dtypes],
        compiler_params=_params(("arbitrary",), 2 * _nbytes((tm, d), jnp.float32),
                                *[_nbytes((tm, d), dt) for dt in out_dtypes]),
        name="residual_layernorm",
    )(x, h, g, b)


def _top_rows(s, k):
    n = s.shape[0]
    ridx = lax.broadcasted_iota(jnp.int32, s.shape, 0).astype(jnp.float32)
    rows = []
    for _ in range(k):
        mx = jnp.max(s, axis=0, keepdims=True)
        rows.append(mx)
        first = jnp.min(jnp.where(s == mx, ridx, float(n)), axis=0, keepdims=True)
        s = jnp.where(ridx == first, -jnp.inf, s)
    return rows


def _peer_route_kernel(q_ref, k1_ref, k2_ref, s1_ref, e1_ref, s2_ref, e2_ref, thr_ref, *, tm):
    nt = (((1,), (1,)), ((), ()))
    q = q_ref[...]
    s1 = lax.dot_general(k1_ref[...], q[:, :PEER_HALF], nt, precision=lax.Precision.HIGHEST,
                         preferred_element_type=jnp.float32)
    s2 = lax.dot_general(k2_ref[...], q[:, PEER_HALF:], nt, precision=lax.Precision.HIGHEST,
                         preferred_element_type=jnp.float32)
    v1 = _top_rows(s1, PEER_TOPK)
    v2 = _top_rows(s2, PEER_TOPK)
    ridx = lax.broadcasted_iota(jnp.int32, (PEER_TOPK, tm), 0)
    v2m = jnp.zeros((PEER_TOPK, tm), jnp.float32)
    for r in range(PEER_TOPK):
        v2m = jnp.where(ridx == r, v2[r], v2m)
    cand = jnp.concatenate([v1[r] + v2m for r in range(PEER_TOPK)], axis=0)
    top = _top_rows(cand, PEER_TOPK)
    thr = top[PEER_TOPK - 1]
    z = jnp.sum(jnp.where(cand >= thr, jnp.exp(cand - top[0]), 0.0), axis=0, keepdims=True)
    s1_ref[...] = s1
    e1_ref[...] = jnp.exp(s1 - v1[0])
    s2_ref[...] = s2
    e2_ref[...] = jnp.exp(s2 - v2[0]) / z
    thr_ref[...] = thr


def _peer_route(qp, keys1, keys2, tm=256):
    s = qp.shape[0]
    big = jax.ShapeDtypeStruct((PEER_HEADS, PEER_N_KEYS, s), jnp.float32)
    big_spec = pl.BlockSpec((None, PEER_N_KEYS, tm), lambda i, h: (h, 0, i))
    key_spec = pl.BlockSpec((PEER_N_KEYS, PEER_HALF), lambda i, h: (0, 0))
    return pl.pallas_call(
        functools.partial(_peer_route_kernel, tm=tm),
        grid=(s // tm, PEER_HEADS),
        in_specs=[pl.BlockSpec((tm, PEER_QUERY_DIM), lambda i, h: (i, h)), key_spec, key_spec],
        out_specs=[big_spec, big_spec, big_spec, big_spec,
                   pl.BlockSpec((None, 1, tm), lambda i, h: (h, 0, i))],
        out_shape=[big, big, big, big, jax.ShapeDtypeStruct((PEER_HEADS, 1, s), jnp.float32)],
        compiler_params=_params(("arbitrary", "arbitrary"),
                                _nbytes((tm, PEER_QUERY_DIM), jnp.float32),
                                2 * _nbytes((PEER_N_KEYS, PEER_HALF), jnp.float32),
                                4 * _nbytes((PEER_N_KEYS, tm), jnp.float32)),
        name="peer_route",
    )(qp, keys1, keys2)


def _gelu(x):
    return 0.5 * x * (1.0 + lax.erf(x * math.sqrt(0.5)))


def _peer_expert_kernel(x_ref, u_ref, v_ref, s1_ref, e1_ref, s2_ref, e2_ref, thr_ref, o_ref, w_ref,
                        *, tm, te):
    e = pl.program_id(1)

    @pl.when(e == 0)
    def _():
        o_ref[...] = jnp.zeros_like(o_ref)

    act = lax.dot_general(u_ref[...], x_ref[...], (((1,), (1,)), ((), ())),
                          preferred_element_type=jnp.float32)
    act = _gelu(act)
    groups = te // PEER_N_KEYS
    for c in range(groups):
        i1 = e * groups + c
        g = jnp.zeros((PEER_N_KEYS, tm), jnp.float32)
        for h in range(PEER_HEADS):
            s1_row = s1_ref[h, pl.ds(i1, 1), :]
            e1_row = e1_ref[h, pl.ds(i1, 1), :]
            chosen = (s1_row + s2_ref[h]) >= thr_ref[h]
            g = g + jnp.where(chosen, e2_ref[h], 0.0) * e1_row
        lo = c * PEER_N_KEYS
        w_ref[lo:lo + PEER_N_KEYS, :] = (g * act[lo:lo + PEER_N_KEYS, :]).astype(w_ref.dtype)
    o_ref[...] += lax.dot_general(w_ref[...], v_ref[...], (((0,), (0,)), ((), ())),
                                  preferred_element_type=jnp.float32)


def _peer_experts(x, u, v, s1, e1, s2, e2, thr, tm=512, te=256):
    s, d = x.shape
    n_exp = u.shape[0]
    big_spec = pl.BlockSpec((PEER_HEADS, PEER_N_KEYS, tm), lambda i, e: (0, 0, i))
    return pl.pallas_call(
        functools.partial(_peer_expert_kernel, tm=tm, te=te),
        grid=(s // tm, n_exp // te),
        in_specs=[pl.BlockSpec((tm, d), lambda i, e: (i, 0)),
                  pl.BlockSpec((te, d), lambda i, e: (e, 0)),
                  pl.BlockSpec((te, d), lambda i, e: (e, 0)),
                  big_spec, big_spec, big_spec, big_spec,
                  pl.BlockSpec((PEER_HEADS, 1, tm), lambda i, e: (0, 0, i))],
        out_specs=pl.BlockSpec((tm, d), lambda i, e: (i, 0)),
        out_shape=jax.ShapeDtypeStruct((s, d), jnp.float32),
        scratch_shapes=[pltpu.VMEM((te, tm), jnp.bfloat16)],
        compiler_params=_params(("arbitrary", "arbitrary"),
                                _nbytes((tm, d), x.dtype), 2 * _nbytes((te, d), u.dtype),
                                4 * _nbytes((PEER_HEADS, PEER_N_KEYS, tm), jnp.float32),
                                _nbytes((tm, d), jnp.float32)),
        name="peer_experts",
    )(x, u, v, s1, e1, s2, e2, thr)


def _layer(x, layer, w_in, gate_b, rel_bias, lq1, lk1, lq2, lk2, subln_g, w_o_moba, w_o_diff, w_out,
           ln1_g, ln1_b, peer_wq, keys1, keys2, peer_u, peer_v, ln2_g, ln2_b):
    bf16 = jnp.bfloat16
    f32 = jnp.float32
    lambda_init = 0.8 - 0.6 * math.exp(-0.3 * layer)
    xb = x.astype(bf16)

    c0 = 2 * MOBA_WIDTH
    c1 = c0 + MOBA_WIDTH + 3 * DIFF_WIDTH
    col_scale = jnp.concatenate([jnp.ones((MOBA_WIDTH,), f32), jnp.full((DIFF_WIDTH,), QK_SCALE_LOG2, f32),
                                 jnp.ones((2 * DIFF_WIDTH,), f32)])
    mqk = _matmul(xb, w_in[:, :c0].astype(bf16), f32, 512, 1024, "proj_moba_qk")
    mid = _matmul(xb, (w_in[:, c0:c1] * col_scale).astype(bf16), bf16, 512, 1024, "proj_attn")
    gates = _matmul(xb, w_in[:, c1:].astype(bf16), f32, 512, 1024, "proj_gates")

    b0, b1 = _bias_tiles(rel_bias, ATTN_TILE)
    qa, ka = _moba_route(mqk)
    y_a = _moba_attention(qa, ka, mid, 0, b0, b1)

    lam = (jnp.exp(jnp.sum(lq1.astype(f32) * lk1.astype(f32)))
           - jnp.exp(jnp.sum(lq2.astype(f32) * lk2.astype(f32))) + lambda_init).reshape(1, 1)
    hb = MOBA_WIDTH // HEAD_DIM
    y_b = _diff_attention(lam, mid, hb, hb + DIFF_WIDTH // HEAD_DIM,
                          (MOBA_WIDTH + 2 * DIFF_WIDTH) // DIFF_V_DIM, b0, b1,
                          subln_g.reshape(1, DIFF_V_DIM).astype(f32), 1.0 - lambda_init)

    merged = _gated_merge(y_a, y_b, w_o_moba.astype(bf16), w_o_diff.astype(bf16), gates,
                          gate_b.reshape(1, 2 * D_MODEL).astype(f32))
    h = _matmul(merged, w_out.astype(bf16), f32, 512, 1024, "proj_out")
    x1, x1b = _residual_ln(x, h, ln1_g.reshape(1, D_MODEL), ln1_b.reshape(1, D_MODEL), (f32, bf16))

    qp = _matmul(x1b, peer_wq.astype(bf16), f32, 512, 1024, "peer_query")
    s1, e1, s2, e2, thr = _peer_route(qp, keys1, keys2)
    h2 = _peer_experts(x1b, peer_u.astype(bf16), peer_v.astype(bf16), s1, e1, s2, e2, thr)
    (x2,) = _residual_ln(x1, h2, ln2_g.reshape(1, D_MODEL), ln2_b.reshape(1, D_MODEL), (f32,))
    return x2


def kernel(x, w_in, gate_b, rel_bias, lambda_q1, lambda_k1, lambda_q2, lambda_k2, subln_g, w_o_moba,
           w_o_diff, w_out, ln1_g, ln1_b, peer_wq, peer_keys1, peer_keys2, peer_u, peer_v, ln2_g, ln2_b):
    b, s, d = x.shape
    assert (b, s, d) == (1, SEQ, D_MODEL) and w_in.shape[0] == DEPTH
    xs = x.reshape(s, d)
    for layer in range(DEPTH):
        xs = _layer(xs, layer, w_in[layer], gate_b[layer], rel_bias, lambda_q1[layer], lambda_k1[layer],
                    lambda_q2[layer], lambda_k2[layer], subln_g[layer], w_o_moba[layer], w_o_diff[layer],
                    w_out[layer], ln1_g[layer], ln1_b[layer], peer_wq[layer], peer_keys1[layer],
                    peer_keys2[layer], peer_u[layer], peer_v[layer], ln2_g[layer], ln2_b[layer])
    return xs.reshape(b, s, d)
```

```python
import functools
import math
from typing import Any, NamedTuple

import jax
import jax.numpy as jnp
import numpy as np
from jax import lax
from jax.experimental import pallas as pl
from jax.experimental.pallas import tpu as pltpu

D_MODEL = 4096
SEQ = 8192
DEPTH = 1
HEAD_DIM = 128
MOBA_HEADS = 16
MOBA_WIDTH = MOBA_HEADS * HEAD_DIM
MOBA_BLOCK = 256
MOBA_TOPK = 3
N_MOBA_BLOCKS = SEQ // MOBA_BLOCK
DIFF_HEADS = 8
DIFF_V_DIM = 2 * HEAD_DIM
DIFF_WIDTH = DIFF_HEADS * DIFF_V_DIM
N_BUCKETS = 32
MAX_DISTANCE = 128
PEER_HEADS = 8
PEER_N_KEYS = 128
PEER_HALF = 128
PEER_QUERY_DIM = 2 * PEER_HALF
PEER_TOPK = 16
DEEPNORM_ALPHA = (2.0 * DEPTH) ** 0.25
LN_EPS = 1e-5
NEG_INF = -1e30

LOG2E = math.log2(math.e)
QK_SCALE_LOG2 = HEAD_DIM ** -0.5 * LOG2E

LANES = 128
MXU_DIM = 256
VMEM_LIMIT_CAP = 60000 * 1024
COMPILER_SCRATCH_BYTES = 8 << 20

ATTN_TILE = 512
ATTN_ROW_CHUNK = 32

_NT = (((1,), (1,)), ((), ()))
_TN = (((0,), (0,)), ((), ()))


def _nbytes(shape, dtype):
    return int(np.prod(shape)) * jnp.dtype(dtype).itemsize


def _params(semantics, pipelined=(), resident=()):
    need = 2 * sum(pipelined) + sum(resident) + COMPILER_SCRATCH_BYTES
    return pltpu.CompilerParams(dimension_semantics=semantics,
                                vmem_limit_bytes=min(need, VMEM_LIMIT_CAP))


def _matmul_kernel(a_ref, b_ref, o_ref):
    o_ref[...] = jnp.dot(a_ref[...], b_ref[...],
                         preferred_element_type=jnp.float32).astype(o_ref.dtype)


def _matmul(a, b, out_dtype, tm, tn, name):
    m, k = a.shape
    n = b.shape[1]
    return pl.pallas_call(
        _matmul_kernel,
        grid=(n // tn, m // tm),
        in_specs=[pl.BlockSpec((tm, k), lambda j, i: (i, 0)),
                  pl.BlockSpec((k, tn), lambda j, i: (0, j))],
        out_specs=pl.BlockSpec((tm, tn), lambda j, i: (i, j)),
        out_shape=jax.ShapeDtypeStruct((m, n), out_dtype),
        compiler_params=_params(("arbitrary", "arbitrary"),
                                (_nbytes((tm, k), a.dtype), _nbytes((k, tn), b.dtype),
                                 _nbytes((tm, tn), out_dtype))),
        name=name,
    )(a, b)


def _moba_route_kernel(q_ref, k_ref, qa_ref, ka_ref, kmean_ref, *, tq):
    i = pl.program_id(1)

    @pl.when(i == 0)
    def _():
        k3 = k_ref[...].reshape(N_MOBA_BLOCKS, MOBA_BLOCK, HEAD_DIM)
        kmean_ref[...] = jnp.zeros_like(kmean_ref)
        kmean_ref[0:N_MOBA_BLOCKS, :] = jnp.sum(k3, axis=1) * (1.0 / MOBA_BLOCK)

    q = q_ref[...]
    gate = lax.dot_general(q, kmean_ref[...], _NT, precision=lax.Precision.HIGHEST,
                           preferred_element_type=jnp.float32)
    lane = lax.broadcasted_iota(jnp.int32, (tq, LANES), 1)
    row = lax.broadcasted_iota(jnp.int32, (tq, LANES), 0)
    own = jnp.right_shift(i * tq + row, int(math.log2(MOBA_BLOCK)))
    lane_f = lane.astype(jnp.float32)
    past = lane < own
    g = jnp.where(past, gate, NEG_INF)
    picked = jnp.zeros((tq, LANES), jnp.float32)
    for _ in range(MOBA_TOPK):
        mx = jnp.max(g, axis=1, keepdims=True)
        first = jnp.min(jnp.where(g == mx, lane_f, float(LANES)), axis=1, keepdims=True)
        hit = lane_f == first
        picked = jnp.where(hit, 1.0, picked)
        g = jnp.where(hit, -jnp.inf, g)
    allowed = jnp.where(past, picked, 0.0)
    allowed = jnp.where(lane == own, 1.0, allowed)
    blocked = jnp.where(lane < N_MOBA_BLOCKS, 1.0 - allowed, 0.0)
    qa_ref[:, :HEAD_DIM] = (q * QK_SCALE_LOG2).astype(qa_ref.dtype)
    qa_ref[:, HEAD_DIM:] = blocked.astype(qa_ref.dtype)
    kb = k_ref[pl.ds(pl.multiple_of(i * tq, tq), tq), :]
    ka_ref[:, :HEAD_DIM] = kb.astype(ka_ref.dtype)
    ka_ref[:, HEAD_DIM:] = jnp.where(lane == own, NEG_INF, 0.0).astype(ka_ref.dtype)


def _moba_route(mqk, tq=512):
    s = mqk.shape[0]
    aug = 2 * HEAD_DIM
    out = jax.ShapeDtypeStruct((s, MOBA_HEADS * aug), jnp.bfloat16)
    return pl.pallas_call(
        functools.partial(_moba_route_kernel, tq=tq),
        grid=(MOBA_HEADS, s // tq),
        in_specs=[pl.BlockSpec((tq, HEAD_DIM), lambda h, i: (i, h)),
                  pl.BlockSpec((s, HEAD_DIM), lambda h, i: (0, MOBA_HEADS + h))],
        out_specs=[pl.BlockSpec((tq, aug), lambda h, i: (i, h)),
                   pl.BlockSpec((tq, aug), lambda h, i: (i, h))],
        out_shape=[out, out],
        scratch_shapes=[pltpu.VMEM((LANES, HEAD_DIM), jnp.float32)],
        compiler_params=_params(("arbitrary", "arbitrary"),
                                (_nbytes((tq, HEAD_DIM), jnp.float32),
                                 _nbytes((s, HEAD_DIM), jnp.float32),
                                 2 * _nbytes((tq, aug), jnp.bfloat16))),
        name="moba_route",
    )(mqk, mqk)


def _t5_bucket(dist, xp):
    n = xp.maximum(dist, 0)
    max_exact = N_BUCKETS // 2
    nf = xp.maximum(n, 1).astype(xp.float32)
    large = max_exact + (xp.log(nf / max_exact) / math.log(MAX_DISTANCE / max_exact)
                         * (N_BUCKETS - max_exact)).astype(xp.int32)
    large = xp.minimum(large, N_BUCKETS - 1)
    return xp.where(n < max_exact, n, large)


def _toeplitz(g, t):
    heads = g.shape[0]
    rows = jnp.tile(g, (1, t))[:, :t * (2 * t - 1)].reshape(heads, t, 2 * t - 1)
    return rows[:, :, :t]


def _bias_tiles(rel_bias, t):
    assert int(_t5_bucket(np.array([t + 1]), np)[0]) == N_BUCKETS - 1
    tab = rel_bias.T.astype(jnp.float32)
    rel = (tab - tab[:, N_BUCKETS - 1:]) * LOG2E
    c = jnp.arange(2 * t, dtype=jnp.int32)
    d0 = jnp.where(c == 0, 0, 2 * t - c)
    g0 = jnp.where((c == 0) | (c > t), rel[:, _t5_bucket(d0, jnp)], NEG_INF)
    d1 = jnp.where(c < t, t - c, 3 * t - c)
    g1 = rel[:, _t5_bucket(d1, jnp)]
    return _toeplitz(g0, t), _toeplitz(g1, t)


class _Stream(NamedTuple):
    q_ref: Any
    k_ref: Any
    v_ref: Any
    b0_ref: Any
    b1_ref: Any
    s_ref: Any
    p_ref: Any
    m_ref: Any
    a_ref: Any
    l_ref: Any
    acc_ref: Any


_STREAM_SCRATCH = ("s_ref", "p_ref", "m_ref", "a_ref", "l_ref", "acc_ref")


def _stream_scratch(t, dv):
    return [pltpu.VMEM((t, t), jnp.float32), pltpu.VMEM((t, t), jnp.bfloat16),
            pltpu.VMEM((t, LANES), jnp.float32), pltpu.VMEM((t, LANES), jnp.float32),
            pltpu.VMEM((t, LANES), jnp.float32), pltpu.VMEM((t, dv), jnp.float32)]


def _stream_scratch_bytes(t, dv):
    return (_nbytes((t, t), jnp.float32) + _nbytes((t, t), jnp.bfloat16)
            + 3 * _nbytes((t, LANES), jnp.float32) + _nbytes((t, dv), jnp.float32))


def _attn_step(streams, j, which_bias, t):
    off = pl.multiple_of(j * t, t)
    for st in streams:
        st.s_ref[...] = lax.dot_general(st.q_ref[...], st.k_ref[pl.ds(off, t), :], _NT,
                                        preferred_element_type=jnp.float32)
    for st in streams:
        bias_ref = None if which_bias is None else getattr(st, which_bias)
        for r in range(t // ATTN_ROW_CHUNK):
            rows = slice(r * ATTN_ROW_CHUNK, (r + 1) * ATTN_ROW_CHUNK)
            blocks = []
            for c in range(t // LANES):
                cols = slice(c * LANES, (c + 1) * LANES)
                blk = st.s_ref[rows, cols]
                if bias_ref is not None:
                    blk = blk + bias_ref[rows, cols]
                blocks.append(blk)
            mx = functools.reduce(jnp.maximum, blocks)
            m_prev = st.m_ref[rows, :]
            m_new = jnp.maximum(m_prev, jnp.max(mx, axis=1, keepdims=True))
            alpha = jnp.exp2(m_prev - m_new)
            probs = [jnp.exp2(blk - m_new) for blk in blocks]
            st.m_ref[rows, :] = m_new
            st.a_ref[rows, :] = alpha
            st.l_ref[rows, :] = alpha * st.l_ref[rows, :] + functools.reduce(jnp.add, probs)
            for c, p in enumerate(probs):
                st.p_ref[rows, c * LANES:(c + 1) * LANES] = p.astype(st.p_ref.dtype)
    for st in streams:
        pv = jnp.dot(st.p_ref[...], st.v_ref[pl.ds(off, t), :], preferred_element_type=jnp.float32)
        for c in range(pv.shape[1] // LANES):
            cols = slice(c * LANES, (c + 1) * LANES)
            st.acc_ref[:, cols] = st.a_ref[...] * st.acc_ref[:, cols] + pv[:, cols]


def _attend(streams, i, t):
    for st in streams:
        st.m_ref[...] = jnp.full_like(st.m_ref, -jnp.inf)
        st.l_ref[...] = jnp.zeros_like(st.l_ref)
        st.acc_ref[...] = jnp.zeros_like(st.acc_ref)

    def far(j, carry):
        _attn_step(streams, j, None, t)
        return carry

    lax.fori_loop(0, jnp.maximum(i - 1, 0), far, 0)

    @pl.when(i > 0)
    def _():
        _attn_step(streams, i - 1, "b1_ref", t)

    _attn_step(streams, i, "b0_ref", t)
    return [st.acc_ref[...] / jnp.sum(st.l_ref[...], axis=1, keepdims=True) for st in streams]


def _moba_attn_kernel(qa_ref, qb_ref, ka_ref, kb_ref, va_ref, vb_ref, b0a_ref, b0b_ref, b1a_ref, b1b_ref,
                      o_ref, *scratch, t):
    n = len(_STREAM_SCRATCH)
    streams = [_Stream(qa_ref, ka_ref, va_ref, b0a_ref, b1a_ref, *scratch[:n]),
               _Stream(qb_ref, kb_ref, vb_ref, b0b_ref, b1b_ref, *scratch[n:])]
    oa, ob = _attend(streams, pl.program_id(1), t)
    o_ref[:, :HEAD_DIM] = oa.astype(o_ref.dtype)
    o_ref[:, HEAD_DIM:] = ob.astype(o_ref.dtype)


def _moba_attention(qa, ka, v, v_col0, b0, b1, t=ATTN_TILE):
    s = qa.shape[0]
    aug = 2 * HEAD_DIM

    def pair(spec_of_head):
        return [spec_of_head(0), spec_of_head(1)]

    return pl.pallas_call(
        functools.partial(_moba_attn_kernel, t=t),
        grid=(MOBA_HEADS // 2, s // t),
        in_specs=(pair(lambda d: pl.BlockSpec((t, aug), lambda g, i: (i, 2 * g + d)))
                  + pair(lambda d: pl.BlockSpec((s, aug), lambda g, i: (0, 2 * g + d)))
                  + pair(lambda d: pl.BlockSpec((s, HEAD_DIM), lambda g, i: (0, v_col0 + 2 * g + d)))
                  + pair(lambda d: pl.BlockSpec((None, t, t), lambda g, i: (2 * g + d, 0, 0)))
                  + pair(lambda d: pl.BlockSpec((None, t, t), lambda g, i: (2 * g + d, 0, 0)))),
        out_specs=pl.BlockSpec((t, 2 * HEAD_DIM), lambda g, i: (i, g)),
        out_shape=jax.ShapeDtypeStruct((s, MOBA_WIDTH), jnp.bfloat16),
        scratch_shapes=2 * _stream_scratch(t, HEAD_DIM),
        compiler_params=_params(("arbitrary", "arbitrary"),
                                (2 * _nbytes((t, aug), jnp.bfloat16), 2 * _nbytes((s, aug), jnp.bfloat16),
                                 2 * _nbytes((s, HEAD_DIM), jnp.bfloat16),
                                 4 * _nbytes((t, t), jnp.float32),
                                 _nbytes((t, 2 * HEAD_DIM), jnp.bfloat16)),
                                (2 * _stream_scratch_bytes(t, HEAD_DIM),)),
        name="moba_attention",
    )(qa, qa, ka, ka, v, v, b0, b0, b1, b1)


def _diff_attn_kernel(lam_ref, q1_ref, q2_ref, k1_ref, k2_ref, v_ref, b0_ref, b1_ref, g_ref, o_ref,
                      *scratch, t, out_scale):
    n = len(_STREAM_SCRATCH)
    streams = [_Stream(q1_ref, k1_ref, v_ref, b0_ref, b1_ref, *scratch[:n]),
               _Stream(q2_ref, k2_ref, v_ref, b0_ref, b1_ref, *scratch[n:])]
    o1, o2 = _attend(streams, pl.program_id(1), t)
    y = o1 - lam_ref[0, 0] * o2
    y = y * lax.rsqrt(jnp.mean(y * y, axis=1, keepdims=True) + LN_EPS) * g_ref[...] * out_scale
    o_ref[...] = y.astype(o_ref.dtype)


def _diff_attention(lam, proj, q_col0, k_col0, v_col0, b0, b1, subln_g, out_scale, t=ATTN_TILE):
    s = proj.shape[0]
    return pl.pallas_call(
        functools.partial(_diff_attn_kernel, t=t, out_scale=out_scale),
        grid=(DIFF_HEADS, s // t),
        in_specs=[pl.BlockSpec(memory_space=pltpu.SMEM),
                  pl.BlockSpec((t, HEAD_DIM), lambda h, i: (i, q_col0 + 2 * h)),
                  pl.BlockSpec((t, HEAD_DIM), lambda h, i: (i, q_col0 + 2 * h + 1)),
                  pl.BlockSpec((s, HEAD_DIM), lambda h, i: (0, k_col0 + 2 * h)),
                  pl.BlockSpec((s, HEAD_DIM), lambda h, i: (0, k_col0 + 2 * h + 1)),
                  pl.BlockSpec((s, DIFF_V_DIM), lambda h, i: (0, v_col0 + h)),
                  pl.BlockSpec((None, t, t), lambda h, i: (MOBA_HEADS + h, 0, 0)),
                  pl.BlockSpec((None, t, t), lambda h, i: (MOBA_HEADS + h, 0, 0)),
                  pl.BlockSpec((1, DIFF_V_DIM), lambda h, i: (0, 0))],
        out_specs=pl.BlockSpec((t, DIFF_V_DIM), lambda h, i: (i, h)),
        out_shape=jax.ShapeDtypeStruct((s, DIFF_WIDTH), jnp.bfloat16),
        scratch_shapes=2 * _stream_scratch(t, DIFF_V_DIM),
        compiler_params=_params(("arbitrary", "arbitrary"),
                                (2 * _nbytes((t, HEAD_DIM), jnp.bfloat16),
                                 2 * _nbytes((s, HEAD_DIM), jnp.bfloat16),
                                 _nbytes((s, DIFF_V_DIM), jnp.bfloat16),
                                 2 * _nbytes((t, t), jnp.float32),
                                 _nbytes((t, DIFF_V_DIM), jnp.bfloat16)),
                                (2 * _stream_scratch_bytes(t, DIFF_V_DIM),)),
        name="diff_attention",
    )(lam, proj, proj, proj, proj, proj, b0, b1, subln_g)


def _sigmoid(x):
    return 1.0 / (1.0 + jnp.exp(-x))


def _merge_kernel(ya_ref, yb_ref, wa_ref, wb_ref, ga_ref, gb_ref, ba_ref, bb_ref, o_ref):
    pa = jnp.dot(ya_ref[...], wa_ref[...], preferred_element_type=jnp.float32)
    pb = jnp.dot(yb_ref[...], wb_ref[...], preferred_element_type=jnp.float32)
    merged = (_sigmoid(ga_ref[...] + ba_ref[...]) * pa + _sigmoid(gb_ref[...] + bb_ref[...]) * pb)
    o_ref[...] = merged.astype(o_ref.dtype)


def _gated_merge(ya, yb, wa, wb, gates, gate_b, tm=512, tn=1024):
    s, k = ya.shape
    n = wa.shape[1]
    nb = n // tn
    return pl.pallas_call(
        _merge_kernel,
        grid=(nb, s // tm),
        in_specs=[pl.BlockSpec((tm, k), lambda j, i: (i, 0)),
                  pl.BlockSpec((tm, k), lambda j, i: (i, 0)),
                  pl.BlockSpec((k, tn), lambda j, i: (0, j)),
                  pl.BlockSpec((k, tn), lambda j, i: (0, j)),
                  pl.BlockSpec((tm, tn), lambda j, i: (i, j)),
                  pl.BlockSpec((tm, tn), lambda j, i: (i, nb + j)),
                  pl.BlockSpec((1, tn), lambda j, i: (0, j)),
                  pl.BlockSpec((1, tn), lambda j, i: (0, nb + j))],
        out_specs=pl.BlockSpec((tm, tn), lambda j, i: (i, j)),
        out_shape=jax.ShapeDtypeStruct((s, n), jnp.bfloat16),
        compiler_params=_params(("arbitrary", "arbitrary"),
                                (2 * _nbytes((tm, k), ya.dtype), 2 * _nbytes((k, tn), wa.dtype),
                                 2 * _nbytes((tm, tn), gates.dtype),
                                 _nbytes((tm, tn), jnp.bfloat16))),
        name="gated_merge",
    )(ya, yb, wa, wb, gates, gates, gate_b, gate_b)


def _residual_ln_kernel(x_ref, h_ref, g_ref, b_ref, *o_refs):
    z = DEEPNORM_ALPHA * x_ref[...] + h_ref[...]
    mu = jnp.mean(z, axis=1, keepdims=True)
    zc = z - mu
    var = jnp.mean(zc * zc, axis=1, keepdims=True)
    y = zc * lax.rsqrt(var + LN_EPS) * g_ref[...] + b_ref[...]
    for o_ref in o_refs:
        o_ref[...] = y.astype(o_ref.dtype)


def _residual_ln(x, h, g, b, out_dtypes, tm=256):
    s, d = x.shape
    row = pl.BlockSpec((tm, d), lambda i: (i, 0))
    vec = pl.BlockSpec((1, d), lambda i: (0, 0))
    return pl.pallas_call(
        _residual_ln_kernel,
        grid=(s // tm,),
        in_specs=[row, row, vec, vec],
        out_specs=[row for _ in out_dtypes],
        out_shape=[jax.ShapeDtypeStruct((s, d), dt) for dt in out_dtypes],
        compiler_params=_params(("arbitrary",),
                                [2 * _nbytes((tm, d), jnp.float32)]
                                + [_nbytes((tm, d), dt) for dt in out_dtypes]),
        name="residual_layernorm",
    )(x, h, g, b)


def _top_rows(s, k):
    n = s.shape[0]
    ridx = lax.broadcasted_iota(jnp.int32, s.shape, 0).astype(jnp.float32)
    rows = []
    for _ in range(k):
        mx = jnp.max(s, axis=0, keepdims=True)
        rows.append(mx)
        first = jnp.min(jnp.where(s == mx, ridx, float(n)), axis=0, keepdims=True)
        s = jnp.where(ridx == first, -jnp.inf, s)
    return rows


def _peer_route_kernel(q_ref, k1_ref, k2_ref, s1_ref, e1_ref, s2_ref, e2_ref, thr_ref, *, tm):
    q = q_ref[...]
    s1 = lax.dot_general(k1_ref[...], q[:, :PEER_HALF], _NT, precision=lax.Precision.HIGHEST,
                         preferred_element_type=jnp.float32)
    s2 = lax.dot_general(k2_ref[...], q[:, PEER_HALF:], _NT, precision=lax.Precision.HIGHEST,
                         preferred_element_type=jnp.float32)
    v1 = _top_rows(s1, PEER_TOPK)
    v2 = _top_rows(s2, PEER_TOPK)
    ridx = lax.broadcasted_iota(jnp.int32, (PEER_TOPK, tm), 0)
    v2m = jnp.zeros((PEER_TOPK, tm), jnp.float32)
    for r in range(PEER_TOPK):
        v2m = jnp.where(ridx == r, v2[r], v2m)
    cand = jnp.concatenate([v1[r] + v2m for r in range(PEER_TOPK)], axis=0)
    top = _top_rows(cand, PEER_TOPK)
    thr = top[PEER_TOPK - 1]
    z = jnp.sum(jnp.where(cand >= thr, jnp.exp(cand - top[0]), 0.0), axis=0, keepdims=True)
    s1_ref[...] = s1
    e1_ref[...] = jnp.exp(s1 - v1[0])
    s2_ref[...] = s2
    e2_ref[...] = jnp.exp(s2 - v2[0]) / z
    thr_ref[...] = thr


def _peer_route(qp, keys1, keys2, tm=256):
    s = qp.shape[0]
    big = jax.ShapeDtypeStruct((PEER_HEADS, PEER_N_KEYS, s), jnp.float32)
    big_spec = pl.BlockSpec((None, PEER_N_KEYS, tm), lambda i, h: (h, 0, i))
    key_spec = pl.BlockSpec((PEER_N_KEYS, PEER_HALF), lambda i, h: (0, 0))
    return pl.pallas_call(
        functools.partial(_peer_route_kernel, tm=tm),
        grid=(s // tm, PEER_HEADS),
        in_specs=[pl.BlockSpec((tm, PEER_QUERY_DIM), lambda i, h: (i, h)), key_spec, key_spec],
        out_specs=[big_spec, big_spec, big_spec, big_spec,
                   pl.BlockSpec((None, 1, tm), lambda i, h: (h, 0, i))],
        out_shape=[big, big, big, big, jax.ShapeDtypeStruct((PEER_HEADS, 1, s), jnp.float32)],
        compiler_params=_params(("arbitrary", "arbitrary"),
                                (_nbytes((tm, PEER_QUERY_DIM), jnp.float32),
                                 2 * _nbytes((PEER_N_KEYS, PEER_HALF), jnp.float32),
                                 4 * _nbytes((PEER_N_KEYS, tm), jnp.float32))),
        name="peer_route",
    )(qp, keys1, keys2)


def _gelu(x):
    return 0.5 * x * (1.0 + lax.erf(x * math.sqrt(0.5)))


def _peer_expert_kernel(x_ref, u_ref, v_ref, s1_ref, e1_ref, s2_ref, e2_ref, thr_ref, o_ref, *w_refs,
                        tm, te):
    e = pl.program_id(1)

    @pl.when(e == 0)
    def _():
        o_ref[...] = jnp.zeros_like(o_ref)

    groups = MXU_DIM // PEER_N_KEYS
    outs = []
    for chain, w_ref in enumerate(w_refs):
        base = chain * MXU_DIM
        act = lax.dot_general(u_ref[base:base + MXU_DIM, :], x_ref[...], _NT,
                              preferred_element_type=jnp.float32)
        act = _gelu(act)
        for c in range(groups):
            i1 = e * (te // PEER_N_KEYS) + chain * groups + c
            g = jnp.zeros((PEER_N_KEYS, tm), jnp.float32)
            for h in range(PEER_HEADS):
                s1_row = s1_ref[h, pl.ds(i1, 1), :]
                e1_row = e1_ref[h, pl.ds(i1, 1), :]
                chosen = (s1_row + s2_ref[h]) >= thr_ref[h]
                g = g + jnp.where(chosen, e2_ref[h], 0.0) * e1_row
            lo = c * PEER_N_KEYS
            w_ref[lo:lo + PEER_N_KEYS, :] = (g * act[lo:lo + PEER_N_KEYS, :]).astype(w_ref.dtype)
        outs.append(lax.dot_general(w_ref[...], v_ref[base:base + MXU_DIM, :], _TN,
                                    preferred_element_type=jnp.float32))
    o_ref[...] += functools.reduce(jnp.add, outs)


def _peer_experts(x, u, v, s1, e1, s2, e2, thr, tm=512, te=512):
    s, d = x.shape
    n_exp = u.shape[0]
    once = pl.Buffered(1)
    big_spec = pl.BlockSpec((PEER_HEADS, PEER_N_KEYS, tm), lambda i, e: (0, 0, i), pipeline_mode=once)
    return pl.pallas_call(
        functools.partial(_peer_expert_kernel, tm=tm, te=te),
        grid=(s // tm, n_exp // te),
        in_specs=[pl.BlockSpec((tm, d), lambda i, e: (i, 0), pipeline_mode=once),
                  pl.BlockSpec((te, d), lambda i, e: (e, 0)),
                  pl.BlockSpec((te, d), lambda i, e: (e, 0)),
                  big_spec, big_spec, big_spec, big_spec,
                  pl.BlockSpec((PEER_HEADS, 1, tm), lambda i, e: (0, 0, i), pipeline_mode=once)],
        out_specs=pl.BlockSpec((tm, d), lambda i, e: (i, 0)),
        out_shape=jax.ShapeDtypeStruct((s, d), jnp.float32),
        scratch_shapes=[pltpu.VMEM((MXU_DIM, tm), jnp.bfloat16) for _ in range(te // MXU_DIM)],
        compiler_params=_params(("arbitrary", "arbitrary"),
                                (2 * _nbytes((te, d), u.dtype), _nbytes((tm, d), jnp.float32)),
                                (_nbytes((tm, d), x.dtype),
                                 4 * _nbytes((PEER_HEADS, PEER_N_KEYS, tm), jnp.float32),
                                 _nbytes((te, tm), jnp.bfloat16))),
        name="peer_experts",
    )(x, u, v, s1, e1, s2, e2, thr)


def _layer(x, layer, w_in, gate_b, rel_bias, lq1, lk1, lq2, lk2, subln_g, w_o_moba, w_o_diff, w_out,
           ln1_g, ln1_b, peer_wq, keys1, keys2, peer_u, peer_v, ln2_g, ln2_b):
    bf16 = jnp.bfloat16
    f32 = jnp.float32
    lambda_init = 0.8 - 0.6 * math.exp(-0.3 * layer)
    xb = x.astype(bf16)

    c0 = 2 * MOBA_WIDTH
    c1 = c0 + MOBA_WIDTH + 3 * DIFF_WIDTH
    col_scale = jnp.concatenate([jnp.ones((MOBA_WIDTH,), f32), jnp.full((DIFF_WIDTH,), QK_SCALE_LOG2, f32),
                                 jnp.ones((2 * DIFF_WIDTH,), f32)])
    mqk = _matmul(xb, w_in[:, :c0].astype(bf16), f32, 512, 1024, "proj_moba_qk")
    mid = _matmul(xb, (w_in[:, c0:c1] * col_scale).astype(bf16), bf16, 512, 1024, "proj_attn")
    gates = _matmul(xb, w_in[:, c1:].astype(bf16), f32, 512, 1024, "proj_gates")

    b0, b1 = _bias_tiles(rel_bias, ATTN_TILE)
    qa, ka = _moba_route(mqk)
    y_a = _moba_attention(qa, ka, mid, 0, b0, b1)

    lam = (jnp.exp(jnp.sum(lq1.astype(f32) * lk1.astype(f32)))
           - jnp.exp(jnp.sum(lq2.astype(f32) * lk2.astype(f32))) + lambda_init).reshape(1, 1)
    hb = MOBA_WIDTH // HEAD_DIM
    y_b = _diff_attention(lam, mid, hb, hb + DIFF_WIDTH // HEAD_DIM,
                          (MOBA_WIDTH + 2 * DIFF_WIDTH) // DIFF_V_DIM, b0, b1,
                          subln_g.reshape(1, DIFF_V_DIM).astype(f32), 1.0 - lambda_init)

    merged = _gated_merge(y_a, y_b, w_o_moba.astype(bf16), w_o_diff.astype(bf16), gates,
                          gate_b.reshape(1, 2 * D_MODEL).astype(f32))
    h = _matmul(merged, w_out.astype(bf16), f32, 512, 1024, "proj_out")
    x1, x1b = _residual_ln(x, h, ln1_g.reshape(1, D_MODEL), ln1_b.reshape(1, D_MODEL), (f32, bf16))

    qp = _matmul(x1b, peer_wq.astype(bf16), f32, 512, 1024, "peer_query")
    s1, e1, s2, e2, thr = _peer_route(qp, keys1, keys2)
    h2 = _peer_experts(x1b, peer_u.astype(bf16), peer_v.astype(bf16), s1, e1, s2, e2, thr)
    (x2,) = _residual_ln(x1, h2, ln2_g.reshape(1, D_MODEL), ln2_b.reshape(1, D_MODEL), (f32,))
    return x2


def kernel(x, w_in, gate_b, rel_bias, lambda_q1, lambda_k1, lambda_q2, lambda_k2, subln_g, w_o_moba,
           w_o_diff, w_out, ln1_g, ln1_b, peer_wq, peer_keys1, peer_keys2, peer_u, peer_v, ln2_g, ln2_b):
    b, s, d = x.shape
    assert (b, s, d) == (1, SEQ, D_MODEL) and w_in.shape[0] == DEPTH
    xs = x.reshape(s, d)
    for layer in range(DEPTH):
        xs = _layer(xs, layer, w_in[layer], gate_b[layer], rel_bias, lambda_q1[layer], lambda_k1[layer],
                    lambda_q2[layer], lambda_k2[layer], subln_g[layer], w_o_moba[layer], w_o_diff[layer],
                    w_out[layer], ln1_g[layer], ln1_b[layer], peer_wq[layer], peer_keys1[layer],
                    peer_keys2[layer], peer_u[layer], peer_v[layer], ln2_g[layer], ln2_b[layer])
    return xs.reshape(b, s, d)
```

```python
import functools
import math
from typing import Any, NamedTuple

import jax
import jax.numpy as jnp
import numpy as np
from jax import lax
from jax.experimental import pallas as pl
from jax.experimental.pallas import tpu as pltpu

D_MODEL = 4096
SEQ = 8192
DEPTH = 1
HEAD_DIM = 128
MOBA_HEADS = 16
MOBA_WIDTH = MOBA_HEADS * HEAD_DIM
MOBA_BLOCK = 256
MOBA_TOPK = 3
N_MOBA_BLOCKS = SEQ // MOBA_BLOCK
DIFF_HEADS = 8
DIFF_V_DIM = 2 * HEAD_DIM
DIFF_WIDTH = DIFF_HEADS * DIFF_V_DIM
N_BUCKETS = 32
MAX_DISTANCE = 128
PEER_HEADS = 8
PEER_N_KEYS = 128
PEER_HALF = 128
PEER_QUERY_DIM = 2 * PEER_HALF
PEER_TOPK = 16
DEEPNORM_ALPHA = (2.0 * DEPTH) ** 0.25
LN_EPS = 1e-5
NEG_INF = -1e30

LOG2E = math.log2(math.e)
QK_SCALE_LOG2 = HEAD_DIM ** -0.5 * LOG2E

LANES = 128
MXU_DIM = 256
VMEM_LIMIT_CAP = 60000 * 1024
COMPILER_SCRATCH_BYTES = 8 << 20

ATTN_TILE = 512
ATTN_ROW_CHUNK = 32

_NT = (((1,), (1,)), ((), ()))
_TN = (((0,), (0,)), ((), ()))


def _nbytes(shape, dtype):
    return int(np.prod(shape)) * jnp.dtype(dtype).itemsize


def _params(semantics, pipelined=(), resident=()):
    need = 2 * sum(pipelined) + sum(resident) + COMPILER_SCRATCH_BYTES
    return pltpu.CompilerParams(dimension_semantics=semantics,
                                vmem_limit_bytes=min(need, VMEM_LIMIT_CAP))


_NN = (((1,), (0,)), ((), ()))


def _matmul_kernel(a_ref, b_ref, *rest):
    *scale, o_ref = rest
    acc = lax.dot_general(a_ref[...], b_ref[...], _NN, preferred_element_type=jnp.float32)
    for s_ref in scale:
        acc = acc * s_ref[...]
    o_ref[...] = acc.astype(o_ref.dtype)


def _matmul(a, w, col0, n, out_dtype, tm, tn, name, col_scale=None):
    m, k = a.shape
    c0 = col0 // tn
    scale_specs = [] if col_scale is None else [pl.BlockSpec((1, tn), lambda j, i: (0, j))]
    scale_args = [] if col_scale is None else [col_scale]
    return pl.pallas_call(
        _matmul_kernel,
        grid=(n // tn, m // tm),
        in_specs=[pl.BlockSpec((tm, k), lambda j, i: (i, 0)),
                  pl.BlockSpec((k, tn), lambda j, i: (0, c0 + j))] + scale_specs,
        out_specs=pl.BlockSpec((tm, tn), lambda j, i: (i, j)),
        out_shape=jax.ShapeDtypeStruct((m, n), out_dtype),
        compiler_params=_params(("arbitrary", "arbitrary"),
                                (_nbytes((tm, k), a.dtype), _nbytes((k, tn), w.dtype),
                                 _nbytes((tm, tn), out_dtype))),
        name=name,
    )(a, w, *scale_args)


def _moba_route_kernel(q_ref, k_ref, qa_ref, ka_ref, kmean_ref, *, tq):
    i = pl.program_id(1)

    @pl.when(i == 0)
    def _():
        k3 = k_ref[...].reshape(N_MOBA_BLOCKS, MOBA_BLOCK, HEAD_DIM)
        kmean_ref[...] = jnp.zeros_like(kmean_ref)
        kmean_ref[0:N_MOBA_BLOCKS, :] = jnp.sum(k3, axis=1) * (1.0 / MOBA_BLOCK)

    q = q_ref[...]
    gate = lax.dot_general(q, kmean_ref[...], _NT, precision=lax.Precision.HIGHEST,
                           preferred_element_type=jnp.float32)
    lane = lax.broadcasted_iota(jnp.int32, (tq, LANES), 1)
    row = lax.broadcasted_iota(jnp.int32, (tq, LANES), 0)
    own = jnp.right_shift(i * tq + row, int(math.log2(MOBA_BLOCK)))
    lane_f = lane.astype(jnp.float32)
    past = lane < own
    g = jnp.where(past, gate, NEG_INF)
    picked = jnp.zeros((tq, LANES), jnp.float32)
    for _ in range(MOBA_TOPK):
        mx = jnp.max(g, axis=1, keepdims=True)
        first = jnp.min(jnp.where(g == mx, lane_f, float(LANES)), axis=1, keepdims=True)
        hit = lane_f == first
        picked = jnp.where(hit, 1.0, picked)
        g = jnp.where(hit, -jnp.inf, g)
    allowed = jnp.where(past, picked, 0.0)
    allowed = jnp.where(lane == own, 1.0, allowed)
    blocked = jnp.where(lane < N_MOBA_BLOCKS, 1.0 - allowed, 0.0)
    qa_ref[:, :HEAD_DIM] = (q * QK_SCALE_LOG2).astype(qa_ref.dtype)
    qa_ref[:, HEAD_DIM:] = blocked.astype(qa_ref.dtype)
    kb = k_ref[pl.ds(pl.multiple_of(i * tq, tq), tq), :]
    ka_ref[:, :HEAD_DIM] = kb.astype(ka_ref.dtype)
    ka_ref[:, HEAD_DIM:] = jnp.where(lane == own, NEG_INF, 0.0).astype(ka_ref.dtype)


def _moba_route(mqk, tq=512):
    s = mqk.shape[0]
    aug = 2 * HEAD_DIM
    out = jax.ShapeDtypeStruct((s, MOBA_HEADS * aug), jnp.bfloat16)
    return pl.pallas_call(
        functools.partial(_moba_route_kernel, tq=tq),
        grid=(MOBA_HEADS, s // tq),
        in_specs=[pl.BlockSpec((tq, HEAD_DIM), lambda h, i: (i, h)),
                  pl.BlockSpec((s, HEAD_DIM), lambda h, i: (0, MOBA_HEADS + h))],
        out_specs=[pl.BlockSpec((tq, aug), lambda h, i: (i, h)),
                   pl.BlockSpec((tq, aug), lambda h, i: (i, h))],
        out_shape=[out, out],
        scratch_shapes=[pltpu.VMEM((LANES, HEAD_DIM), jnp.float32)],
        compiler_params=_params(("arbitrary", "arbitrary"),
                                (_nbytes((tq, HEAD_DIM), jnp.float32),
                                 _nbytes((s, HEAD_DIM), jnp.float32),
                                 2 * _nbytes((tq, aug), jnp.bfloat16))),
        name="moba_route",
    )(mqk, mqk)


def _t5_bucket(dist, xp):
    n = xp.maximum(dist, 0)
    max_exact = N_BUCKETS // 2
    nf = xp.maximum(n, 1).astype(xp.float32)
    large = max_exact + (xp.log(nf / max_exact) / math.log(MAX_DISTANCE / max_exact)
                         * (N_BUCKETS - max_exact)).astype(xp.int32)
    large = xp.minimum(large, N_BUCKETS - 1)
    return xp.where(n < max_exact, n, large)


def _toeplitz(g, t):
    heads = g.shape[0]
    rows = jnp.tile(g, (1, t))[:, :t * (2 * t - 1)].reshape(heads, t, 2 * t - 1)
    return rows[:, :, :t]


def _bias_tiles(rel_bias, t):
    assert int(_t5_bucket(np.array([t + 1]), np)[0]) == N_BUCKETS - 1
    tab = rel_bias.T.astype(jnp.float32)
    rel = (tab - tab[:, N_BUCKETS - 1:]) * LOG2E
    c = jnp.arange(2 * t, dtype=jnp.int32)
    d0 = jnp.where(c == 0, 0, 2 * t - c)
    g0 = jnp.where((c == 0) | (c > t), rel[:, _t5_bucket(d0, jnp)], NEG_INF)
    d1 = jnp.where(c < t, t - c, 3 * t - c)
    g1 = rel[:, _t5_bucket(d1, jnp)]
    return _toeplitz(g0, t), _toeplitz(g1, t)


class _Stream(NamedTuple):
    q_ref: Any
    k_ref: Any
    v_ref: Any
    b0_ref: Any
    b1_ref: Any
    s_ref: Any
    p_ref: Any
    m_ref: Any
    a_ref: Any
    l_ref: Any
    acc_ref: Any


_STREAM_SCRATCH = ("s_ref", "p_ref", "m_ref", "a_ref", "l_ref", "acc_ref")


def _stream_scratch(t, dv):
    return [pltpu.VMEM((t, t), jnp.float32), pltpu.VMEM((t, t), jnp.bfloat16),
            pltpu.VMEM((t, LANES), jnp.float32), pltpu.VMEM((t, LANES), jnp.float32),
            pltpu.VMEM((t, LANES), jnp.float32), pltpu.VMEM((t, dv), jnp.float32)]


def _stream_scratch_bytes(t, dv):
    return (_nbytes((t, t), jnp.float32) + _nbytes((t, t), jnp.bfloat16)
            + 3 * _nbytes((t, LANES), jnp.float32) + _nbytes((t, dv), jnp.float32))


def _attn_step(streams, j, which_bias, t):
    off = pl.multiple_of(j * t, t)
    for st in streams:
        st.s_ref[...] = lax.dot_general(st.q_ref[...], st.k_ref[pl.ds(off, t), :], _NT,
                                        preferred_element_type=jnp.float32)
    for st in streams:
        bias_ref = None if which_bias is None else getattr(st, which_bias)
        for r in range(t // ATTN_ROW_CHUNK):
            rows = slice(r * ATTN_ROW_CHUNK, (r + 1) * ATTN_ROW_CHUNK)
            blocks = []
            for c in range(t // LANES):
                cols = slice(c * LANES, (c + 1) * LANES)
                blk = st.s_ref[rows, cols]
                if bias_ref is not None:
                    blk = blk + bias_ref[rows, cols]
                blocks.append(blk)
            mx = functools.reduce(jnp.maximum, blocks)
            m_prev = st.m_ref[rows, :]
            m_new = jnp.maximum(m_prev, jnp.max(mx, axis=1, keepdims=True))
            alpha = jnp.exp2(m_prev - m_new)
            probs = [jnp.exp2(blk - m_new) for blk in blocks]
            st.m_ref[rows, :] = m_new
            st.a_ref[rows, :] = alpha
            st.l_ref[rows, :] = alpha * st.l_ref[rows, :] + functools.reduce(jnp.add, probs)
            for c, p in enumerate(probs):
                st.p_ref[rows, c * LANES:(c + 1) * LANES] = p.astype(st.p_ref.dtype)
    for st in streams:
        pv = jnp.dot(st.p_ref[...], st.v_ref[pl.ds(off, t), :], preferred_element_type=jnp.float32)
        for c in range(pv.shape[1] // LANES):
            cols = slice(c * LANES, (c + 1) * LANES)
            st.acc_ref[:, cols] = st.a_ref[...] * st.acc_ref[:, cols] + pv[:, cols]


def _attend(streams, i, t):
    for st in streams:
        st.m_ref[...] = jnp.full_like(st.m_ref, -jnp.inf)
        st.l_ref[...] = jnp.zeros_like(st.l_ref)
        st.acc_ref[...] = jnp.zeros_like(st.acc_ref)

    def far(j, carry):
        _attn_step(streams, j, None, t)
        return carry

    lax.fori_loop(0, jnp.maximum(i - 1, 0), far, 0)

    @pl.when(i > 0)
    def _():
        _attn_step(streams, i - 1, "b1_ref", t)

    _attn_step(streams, i, "b0_ref", t)
    return [st.acc_ref[...] / jnp.sum(st.l_ref[...], axis=1, keepdims=True) for st in streams]


def _moba_attn_kernel(qa_ref, qb_ref, ka_ref, kb_ref, va_ref, vb_ref, b0a_ref, b0b_ref, b1a_ref, b1b_ref,
                      o_ref, *scratch, t):
    n = len(_STREAM_SCRATCH)
    streams = [_Stream(qa_ref, ka_ref, va_ref, b0a_ref, b1a_ref, *scratch[:n]),
               _Stream(qb_ref, kb_ref, vb_ref, b0b_ref, b1b_ref, *scratch[n:])]
    oa, ob = _attend(streams, pl.program_id(1), t)
    o_ref[:, :HEAD_DIM] = oa.astype(o_ref.dtype)
    o_ref[:, HEAD_DIM:] = ob.astype(o_ref.dtype)


def _moba_attention(qa, ka, v, v_col0, b0, b1, t=ATTN_TILE):
    s = qa.shape[0]
    aug = 2 * HEAD_DIM

    def pair(spec_of_head):
        return [spec_of_head(0), spec_of_head(1)]

    return pl.pallas_call(
        functools.partial(_moba_attn_kernel, t=t),
        grid=(MOBA_HEADS // 2, s // t),
        in_specs=(pair(lambda d: pl.BlockSpec((t, aug), lambda g, i: (i, 2 * g + d)))
                  + pair(lambda d: pl.BlockSpec((s, aug), lambda g, i: (0, 2 * g + d)))
                  + pair(lambda d: pl.BlockSpec((s, HEAD_DIM), lambda g, i: (0, v_col0 + 2 * g + d)))
                  + pair(lambda d: pl.BlockSpec((None, t, t), lambda g, i: (2 * g + d, 0, 0)))
                  + pair(lambda d: pl.BlockSpec((None, t, t), lambda g, i: (2 * g + d, 0, 0)))),
        out_specs=pl.BlockSpec((t, 2 * HEAD_DIM), lambda g, i: (i, g)),
        out_shape=jax.ShapeDtypeStruct((s, MOBA_WIDTH), jnp.bfloat16),
        scratch_shapes=2 * _stream_scratch(t, HEAD_DIM),
        compiler_params=_params(("arbitrary", "arbitrary"),
                                (2 * _nbytes((t, aug), jnp.bfloat16), 2 * _nbytes((s, aug), jnp.bfloat16),
                                 2 * _nbytes((s, HEAD_DIM), jnp.bfloat16),
                                 4 * _nbytes((t, t), jnp.float32),
                                 _nbytes((t, 2 * HEAD_DIM), jnp.bfloat16)),
                                (2 * _stream_scratch_bytes(t, HEAD_DIM),)),
        name="moba_attention",
    )(qa, qa, ka, ka, v, v, b0, b0, b1, b1)


def _diff_attn_kernel(lam_ref, q1_ref, q2_ref, k1_ref, k2_ref, v_ref, b0_ref, b1_ref, g_ref, o_ref,
                      *scratch, t, out_scale):
    n = len(_STREAM_SCRATCH)
    streams = [_Stream(q1_ref, k1_ref, v_ref, b0_ref, b1_ref, *scratch[:n]),
               _Stream(q2_ref, k2_ref, v_ref, b0_ref, b1_ref, *scratch[n:])]
    o1, o2 = _attend(streams, pl.program_id(1), t)
    y = o1 - lam_ref[0, 0] * o2
    y = y * lax.rsqrt(jnp.mean(y * y, axis=1, keepdims=True) + LN_EPS) * g_ref[...] * out_scale
    o_ref[...] = y.astype(o_ref.dtype)


def _diff_attention(lam, proj, q_col0, k_col0, v_col0, b0, b1, subln_g, out_scale, t=ATTN_TILE):
    s = proj.shape[0]
    return pl.pallas_call(
        functools.partial(_diff_attn_kernel, t=t, out_scale=out_scale),
        grid=(DIFF_HEADS, s // t),
        in_specs=[pl.BlockSpec(memory_space=pltpu.SMEM),
                  pl.BlockSpec((t, HEAD_DIM), lambda h, i: (i, q_col0 + 2 * h)),
                  pl.BlockSpec((t, HEAD_DIM), lambda h, i: (i, q_col0 + 2 * h + 1)),
                  pl.BlockSpec((s, HEAD_DIM), lambda h, i: (0, k_col0 + 2 * h)),
                  pl.BlockSpec((s, HEAD_DIM), lambda h, i: (0, k_col0 + 2 * h + 1)),
                  pl.BlockSpec((s, DIFF_V_DIM), lambda h, i: (0, v_col0 + h)),
                  pl.BlockSpec((None, t, t), lambda h, i: (MOBA_HEADS + h, 0, 0)),
                  pl.BlockSpec((None, t, t), lambda h, i: (MOBA_HEADS + h, 0, 0)),
                  pl.BlockSpec((1, DIFF_V_DIM), lambda h, i: (0, 0))],
        out_specs=pl.BlockSpec((t, DIFF_V_DIM), lambda h, i: (i, h)),
        out_shape=jax.ShapeDtypeStruct((s, DIFF_WIDTH), jnp.bfloat16),
        scratch_shapes=2 * _stream_scratch(t, DIFF_V_DIM),
        compiler_params=_params(("arbitrary", "arbitrary"),
                                (2 * _nbytes((t, HEAD_DIM), jnp.bfloat16),
                                 2 * _nbytes((s, HEAD_DIM), jnp.bfloat16),
                                 _nbytes((s, DIFF_V_DIM), jnp.bfloat16),
                                 2 * _nbytes((t, t), jnp.float32),
                                 _nbytes((t, DIFF_V_DIM), jnp.bfloat16)),
                                (2 * _stream_scratch_bytes(t, DIFF_V_DIM),)),
        name="diff_attention",
    )(lam, proj, proj, proj, proj, proj, b0, b1, subln_g)


def _sigmoid(x):
    return 1.0 / (1.0 + jnp.exp(-x))


def _merge_kernel(ya_ref, yb_ref, wa_ref, wb_ref, ga_ref, gb_ref, ba_ref, bb_ref, o_ref):
    pa = lax.dot_general(ya_ref[...], wa_ref[...], _NN, preferred_element_type=jnp.float32)
    pb = lax.dot_general(yb_ref[...], wb_ref[...], _NN, preferred_element_type=jnp.float32)
    merged = (_sigmoid(ga_ref[...] + ba_ref[...]) * pa + _sigmoid(gb_ref[...] + bb_ref[...]) * pb)
    o_ref[...] = merged.astype(o_ref.dtype)


def _gated_merge(ya, yb, wa, wb, gates, gate_b, tm=512, tn=512):
    s, k = ya.shape
    n = wa.shape[1]
    nb = n // tn
    return pl.pallas_call(
        _merge_kernel,
        grid=(nb, s // tm),
        in_specs=[pl.BlockSpec((tm, k), lambda j, i: (i, 0)),
                  pl.BlockSpec((tm, k), lambda j, i: (i, 0)),
                  pl.BlockSpec((k, tn), lambda j, i: (0, j)),
                  pl.BlockSpec((k, tn), lambda j, i: (0, j)),
                  pl.BlockSpec((tm, tn), lambda j, i: (i, j)),
                  pl.BlockSpec((tm, tn), lambda j, i: (i, nb + j)),
                  pl.BlockSpec((1, tn), lambda j, i: (0, j)),
                  pl.BlockSpec((1, tn), lambda j, i: (0, nb + j))],
        out_specs=pl.BlockSpec((tm, tn), lambda j, i: (i, j)),
        out_shape=jax.ShapeDtypeStruct((s, n), jnp.bfloat16),
        compiler_params=_params(("arbitrary", "arbitrary"),
                                (2 * _nbytes((tm, k), ya.dtype), 2 * _nbytes((k, tn), wa.dtype),
                                 2 * _nbytes((tm, tn), gates.dtype),
                                 _nbytes((tm, tn), jnp.bfloat16))),
        name="gated_merge",
    )(ya, yb, wa, wb, gates, gates, gate_b, gate_b)


def _residual_ln_kernel(x_ref, h_ref, g_ref, b_ref, *o_refs):
    z = DEEPNORM_ALPHA * x_ref[...] + h_ref[...]
    mu = jnp.mean(z, axis=1, keepdims=True)
    zc = z - mu
    var = jnp.mean(zc * zc, axis=1, keepdims=True)
    y = zc * lax.rsqrt(var + LN_EPS) * g_ref[...] + b_ref[...]
    for o_ref in o_refs:
        o_ref[...] = y.astype(o_ref.dtype)


def _residual_ln(x, h, g, b, out_dtypes, tm=256):
    s, d = x.shape
    row = pl.BlockSpec((tm, d), lambda i: (i, 0))
    vec = pl.BlockSpec((1, d), lambda i: (0, 0))
    return pl.pallas_call(
        _residual_ln_kernel,
        grid=(s // tm,),
        in_specs=[row, row, vec, vec],
        out_specs=[row for _ in out_dtypes],
        out_shape=[jax.ShapeDtypeStruct((s, d), dt) for dt in out_dtypes],
        compiler_params=_params(("arbitrary",),
                                [2 * _nbytes((tm, d), jnp.float32)]
                                + [_nbytes((tm, d), dt) for dt in out_dtypes]),
        name="residual_layernorm",
    )(x, h, g, b)


def _top_rows_one_by_one(s, k):
    n = s.shape[0]
    ridx = lax.broadcasted_iota(jnp.int32, s.shape, 0).astype(jnp.float32)
    rows = []
    for _ in range(k):
        mx = jnp.max(s, axis=0, keepdims=True)
        rows.append(mx)
        first = jnp.min(jnp.where(s == mx, ridx, float(n)), axis=0, keepdims=True)
        s = jnp.where(ridx == first, -jnp.inf, s)
    return tuple(rows)


def _top_rows(s, k):
    rows, rest = [], s
    for _ in range(k):
        mx = jnp.max(rest, axis=0, keepdims=True)
        rows.append(mx)
        rest = jnp.where(rest == mx, -jnp.inf, rest)
    removed = jnp.sum(jnp.where(rest == -jnp.inf, 1.0, 0.0), axis=0, keepdims=True)
    return lax.cond(jnp.max(removed) > k, lambda: _top_rows_one_by_one(s, k), lambda: tuple(rows))


def _stack_rows(rows):
    k, tm = len(rows), rows[0].shape[1]
    ridx = lax.broadcasted_iota(jnp.int32, (k, tm), 0)
    out = jnp.zeros((k, tm), rows[0].dtype)
    for r, row in enumerate(rows):
        out = jnp.where(ridx == r, row, out)
    return out


def _peer_route_kernel(q_ref, k1_ref, k2_ref, s1_ref, e1_ref, s2_ref, e2_ref, thr_ref, *, tm):
    q = q_ref[...]
    s1 = lax.dot_general(k1_ref[...], q[:, :PEER_HALF], _NT, precision=lax.Precision.HIGHEST,
                         preferred_element_type=jnp.float32)
    s2 = lax.dot_general(k2_ref[...], q[:, PEER_HALF:], _NT, precision=lax.Precision.HIGHEST,
                         preferred_element_type=jnp.float32)
    v1 = _top_rows(s1, PEER_TOPK)
    v2 = _top_rows(s2, PEER_TOPK)
    v1m = _stack_rows(v1)
    v2m = _stack_rows(v2)
    half = PEER_TOPK // 2
    assert all((a + 1) * (b + 1) > PEER_TOPK for a in range(1, half) for b in range(half, PEER_TOPK))
    assert all((a + 1) * (b + 1) > PEER_TOPK for a in range(half, PEER_TOPK) for b in range(1, PEER_TOPK))
    cand = jnp.concatenate([v1[0] + v2m] + [v1[a] + v2m[:half] for a in range(1, half)]
                           + [v1m[half:] + v2[0]], axis=0)
    top = _top_rows(cand, PEER_TOPK)
    thr = top[PEER_TOPK - 1]
    z = jnp.sum(jnp.where(cand >= thr, jnp.exp(cand - top[0]), 0.0), axis=0, keepdims=True)
    s1_ref[...] = s1
    e1_ref[...] = jnp.exp(s1 - v1[0])
    s2_ref[...] = s2
    e2_ref[...] = jnp.exp(s2 - v2[0]) / z
    thr_ref[...] = thr


def _peer_route(qp, keys1, keys2, tm=512):
    s = qp.shape[0]
    big = jax.ShapeDtypeStruct((PEER_HEADS, PEER_N_KEYS, s), jnp.float32)
    big_spec = pl.BlockSpec((None, PEER_N_KEYS, tm), lambda i, h: (h, 0, i))
    key_spec = pl.BlockSpec((PEER_N_KEYS, PEER_HALF), lambda i, h: (0, 0))
    return pl.pallas_call(
        functools.partial(_peer_route_kernel, tm=tm),
        grid=(s // tm, PEER_HEADS),
        in_specs=[pl.BlockSpec((tm, PEER_QUERY_DIM), lambda i, h: (i, h)), key_spec, key_spec],
        out_specs=[big_spec, big_spec, big_spec, big_spec,
                   pl.BlockSpec((None, 1, tm), lambda i, h: (h, 0, i))],
        out_shape=[big, big, big, big, jax.ShapeDtypeStruct((PEER_HEADS, 1, s), jnp.float32)],
        compiler_params=_params(("arbitrary", "arbitrary"),
                                (_nbytes((tm, PEER_QUERY_DIM), jnp.float32),
                                 2 * _nbytes((PEER_N_KEYS, PEER_HALF), jnp.float32),
                                 4 * _nbytes((PEER_N_KEYS, tm), jnp.float32))),
        name="peer_route",
    )(qp, keys1, keys2)


def _gelu(x):
    return 0.5 * x * (1.0 + lax.erf(x * math.sqrt(0.5)))


def _peer_expert_kernel(x_ref, u_ref, v_ref, s1_ref, e1_ref, s2_ref, e2_ref, thr_ref, o_ref, *w_refs,
                        tm, te):
    e = pl.program_id(1)

    @pl.when(e == 0)
    def _():
        o_ref[...] = jnp.zeros_like(o_ref)

    groups = MXU_DIM // PEER_N_KEYS
    outs = []
    for chain, w_ref in enumerate(w_refs):
        base = chain * MXU_DIM
        act = lax.dot_general(u_ref[base:base + MXU_DIM, :], x_ref[...], _NT,
                              preferred_element_type=jnp.float32)
        act = _gelu(act)
        for c in range(groups):
            i1 = e * (te // PEER_N_KEYS) + chain * groups + c
            g = jnp.zeros((PEER_N_KEYS, tm), jnp.float32)
            for h in range(PEER_HEADS):
                s1_row = s1_ref[h, pl.ds(i1, 1), :]
                e1_row = e1_ref[h, pl.ds(i1, 1), :]
                chosen = (s1_row + s2_ref[h]) >= thr_ref[h]
                g = g + jnp.where(chosen, e2_ref[h], 0.0) * e1_row
            lo = c * PEER_N_KEYS
            w_ref[lo:lo + PEER_N_KEYS, :] = (g * act[lo:lo + PEER_N_KEYS, :]).astype(w_ref.dtype)
        outs.append(lax.dot_general(w_ref[...], v_ref[base:base + MXU_DIM, :], _TN,
                                    preferred_element_type=jnp.float32))
    o_ref[...] += functools.reduce(jnp.add, outs)


def _peer_experts(x, u, v, s1, e1, s2, e2, thr, tm=512, te=512):
    s, d = x.shape
    n_exp = u.shape[0]
    once = pl.Buffered(1)
    big_spec = pl.BlockSpec((PEER_HEADS, PEER_N_KEYS, tm), lambda i, e: (0, 0, i), pipeline_mode=once)
    return pl.pallas_call(
        functools.partial(_peer_expert_kernel, tm=tm, te=te),
        grid=(s // tm, n_exp // te),
        in_specs=[pl.BlockSpec((tm, d), lambda i, e: (i, 0), pipeline_mode=once),
                  pl.BlockSpec((te, d), lambda i, e: (e, 0)),
                  pl.BlockSpec((te, d), lambda i, e: (e, 0)),
                  big_spec, big_spec, big_spec, big_spec,
                  pl.BlockSpec((PEER_HEADS, 1, tm), lambda i, e: (0, 0, i), pipeline_mode=once)],
        out_specs=pl.BlockSpec((tm, d), lambda i, e: (i, 0)),
        out_shape=jax.ShapeDtypeStruct((s, d), jnp.float32),
        scratch_shapes=[pltpu.VMEM((MXU_DIM, tm), jnp.bfloat16) for _ in range(te // MXU_DIM)],
        compiler_params=_params(("arbitrary", "arbitrary"),
                                (2 * _nbytes((te, d), u.dtype), _nbytes((tm, d), jnp.float32)),
                                (_nbytes((tm, d), x.dtype),
                                 4 * _nbytes((PEER_HEADS, PEER_N_KEYS, tm), jnp.float32),
                                 _nbytes((te, tm), jnp.bfloat16))),
        name="peer_experts",
    )(x, u, v, s1, e1, s2, e2, thr)


def _layer(x, layer, w_in, gate_b, rel_bias, lq1, lk1, lq2, lk2, subln_g, w_o_moba, w_o_diff, w_out,
           ln1_g, ln1_b, peer_wq, keys1, keys2, peer_u, peer_v, ln2_g, ln2_b):
    bf16 = jnp.bfloat16
    f32 = jnp.float32
    lambda_init = 0.8 - 0.6 * math.exp(-0.3 * layer)
    xb = x.astype(bf16)

    c0 = 2 * MOBA_WIDTH
    c1 = c0 + MOBA_WIDTH + 3 * DIFF_WIDTH
    col_scale = jnp.concatenate([jnp.ones((MOBA_WIDTH,), f32), jnp.full((DIFF_WIDTH,), QK_SCALE_LOG2, f32),
                                 jnp.ones((2 * DIFF_WIDTH,), f32)]).reshape(1, c1 - c0)
    mqk = _matmul(xb, w_in, 0, c0, f32, 512, 1024, "proj_moba_qk")
    mid = _matmul(xb, w_in, c0, c1 - c0, bf16, 512, 1024, "proj_attn", col_scale)
    gates = _matmul(xb, w_in, c1, 2 * D_MODEL, f32, 512, 1024, "proj_gates")

    b0, b1 = _bias_tiles(rel_bias, ATTN_TILE)
    qa, ka = _moba_route(mqk)
    y_a = _moba_attention(qa, ka, mid, 0, b0, b1)

    lam = (jnp.exp(jnp.sum(lq1.astype(f32) * lk1.astype(f32)))
           - jnp.exp(jnp.sum(lq2.astype(f32) * lk2.astype(f32))) + lambda_init).reshape(1, 1)
    hb = MOBA_WIDTH // HEAD_DIM
    y_b = _diff_attention(lam, mid, hb, hb + DIFF_WIDTH // HEAD_DIM,
                          (MOBA_WIDTH + 2 * DIFF_WIDTH) // DIFF_V_DIM, b0, b1,
                          subln_g.reshape(1, DIFF_V_DIM).astype(f32), 1.0 - lambda_init)

    merged = _gated_merge(y_a, y_b, w_o_moba, w_o_diff, gates,
                          gate_b.reshape(1, 2 * D_MODEL).astype(f32))
    h = _matmul(merged, w_out, 0, D_MODEL, f32, 512, 1024, "proj_out")
    x1, x1b = _residual_ln(x, h, ln1_g.reshape(1, D_MODEL), ln1_b.reshape(1, D_MODEL), (f32, bf16))

    qp = _matmul(x1b, peer_wq, 0, PEER_HEADS * PEER_QUERY_DIM, f32, 512, 1024, "peer_query")
    s1, e1, s2, e2, thr = _peer_route(qp, keys1, keys2)
    h2 = _peer_experts(x1b, peer_u.astype(bf16), peer_v.astype(bf16), s1, e1, s2, e2, thr)
    (x2,) = _residual_ln(x1, h2, ln2_g.reshape(1, D_MODEL), ln2_b.reshape(1, D_MODEL), (f32,))
    return x2


def kernel(x, w_in, gate_b, rel_bias, lambda_q1, lambda_k1, lambda_q2, lambda_k2, subln_g, w_o_moba,
           w_o_diff, w_out, ln1_g, ln1_b, peer_wq, peer_keys1, peer_keys2, peer_u, peer_v, ln2_g, ln2_b):
    b, s, d = x.shape
    assert (b, s, d) == (1, SEQ, D_MODEL) and w_in.shape[0] == DEPTH
    xs = x.reshape(s, d)
    for layer in range(DEPTH):
        xs = _layer(xs, layer, w_in[layer], gate_b[layer], rel_bias, lambda_q1[layer], lambda_k1[layer],
                    lambda_q2[layer], lambda_k2[layer], subln_g[layer], w_o_moba[layer], w_o_diff[layer],
                    w_out[layer], ln1_g[layer], ln1_b[layer], peer_wq[layer], peer_keys1[layer],
                    peer_keys2[layer], peer_u[layer], peer_v[layer], ln2_g[layer], ln2_b[layer])
    return xs.reshape(b, s, d)
```

```python
import functools
import math
from typing import Any, NamedTuple

import jax
import jax.numpy as jnp
import numpy as np
from jax import lax
from jax.experimental import pallas as pl
from jax.experimental.pallas import tpu as pltpu

D_MODEL = 4096
SEQ = 8192
DEPTH = 1
HEAD_DIM = 128
MOBA_HEADS = 16
MOBA_WIDTH = MOBA_HEADS * HEAD_DIM
MOBA_BLOCK = 256
MOBA_TOPK = 3
N_MOBA_BLOCKS = SEQ // MOBA_BLOCK
DIFF_HEADS = 8
DIFF_V_DIM = 2 * HEAD_DIM
DIFF_WIDTH = DIFF_HEADS * DIFF_V_DIM
N_BUCKETS = 32
MAX_DISTANCE = 128
PEER_HEADS = 8
PEER_N_KEYS = 128
PEER_HALF = 128
PEER_QUERY_DIM = 2 * PEER_HALF
PEER_TOPK = 16
DEEPNORM_ALPHA = (2.0 * DEPTH) ** 0.25
LN_EPS = 1e-5
NEG_INF = -1e30

LOG2E = math.log2(math.e)
QK_SCALE_LOG2 = HEAD_DIM ** -0.5 * LOG2E

LANES = 128
MXU_DIM = 256
VMEM_LIMIT_CAP = 60000 * 1024
COMPILER_SCRATCH_BYTES = 8 << 20

ATTN_TILE = 512
BIAS_TILE = LANES
ATTN_ROW_CHUNK = 32

_NT = (((1,), (1,)), ((), ()))
_TN = (((0,), (0,)), ((), ()))


def _nbytes(shape, dtype):
    return int(np.prod(shape)) * jnp.dtype(dtype).itemsize


def _params(semantics, pipelined=(), resident=()):
    need = 2 * sum(pipelined) + sum(resident) + COMPILER_SCRATCH_BYTES
    return pltpu.CompilerParams(dimension_semantics=semantics,
                                vmem_limit_bytes=min(need, VMEM_LIMIT_CAP))


_NN = (((1,), (0,)), ((), ()))


def _matmul_kernel(a_ref, b_ref, *rest):
    *scale, o_ref = rest
    acc = lax.dot_general(a_ref[...], b_ref[...], _NN, preferred_element_type=jnp.float32)
    for s_ref in scale:
        acc = acc * s_ref[...]
    o_ref[...] = acc.astype(o_ref.dtype)


def _matmul(a, w, col0, n, out_dtype, tm, tn, name, col_scale=None):
    m, k = a.shape
    c0 = col0 // tn
    scale_specs = [] if col_scale is None else [pl.BlockSpec((1, tn), lambda j, i: (0, j))]
    scale_args = [] if col_scale is None else [col_scale]
    return pl.pallas_call(
        _matmul_kernel,
        grid=(n // tn, m // tm),
        in_specs=[pl.BlockSpec((tm, k), lambda j, i: (i, 0)),
                  pl.BlockSpec((k, tn), lambda j, i: (0, c0 + j))] + scale_specs,
        out_specs=pl.BlockSpec((tm, tn), lambda j, i: (i, j)),
        out_shape=jax.ShapeDtypeStruct((m, n), out_dtype),
        compiler_params=_params(("arbitrary", "arbitrary"),
                                (_nbytes((tm, k), a.dtype), _nbytes((k, tn), w.dtype),
                                 _nbytes((tm, tn), out_dtype))),
        name=name,
    )(a, w, *scale_args)


def _moba_route_kernel(q_ref, k_ref, qa_ref, ka_ref, kmean_ref, *, tq):
    i = pl.program_id(1)

    @pl.when(i == 0)
    def _():
        k3 = k_ref[...].reshape(N_MOBA_BLOCKS, MOBA_BLOCK, HEAD_DIM)
        kmean_ref[...] = jnp.zeros_like(kmean_ref)
        kmean_ref[0:N_MOBA_BLOCKS, :] = jnp.sum(k3, axis=1) * (1.0 / MOBA_BLOCK)

    q = q_ref[...]
    gate = lax.dot_general(q, kmean_ref[...], _NT, precision=lax.Precision.HIGHEST,
                           preferred_element_type=jnp.float32)
    lane = lax.broadcasted_iota(jnp.int32, (tq, LANES), 1)
    row = lax.broadcasted_iota(jnp.int32, (tq, LANES), 0)
    own = jnp.right_shift(i * tq + row, int(math.log2(MOBA_BLOCK)))
    lane_f = lane.astype(jnp.float32)
    past = lane < own
    g = jnp.where(past, gate, NEG_INF)
    picked = jnp.zeros((tq, LANES), jnp.float32)
    for _ in range(MOBA_TOPK):
        mx = jnp.max(g, axis=1, keepdims=True)
        first = jnp.min(jnp.where(g == mx, lane_f, float(LANES)), axis=1, keepdims=True)
        hit = lane_f == first
        picked = jnp.where(hit, 1.0, picked)
        g = jnp.where(hit, -jnp.inf, g)
    allowed = jnp.where(past, picked, 0.0)
    allowed = jnp.where(lane == own, 1.0, allowed)
    blocked = jnp.where(lane < N_MOBA_BLOCKS, 1.0 - allowed, 0.0)
    qa_ref[:, :HEAD_DIM] = (q * QK_SCALE_LOG2).astype(qa_ref.dtype)
    qa_ref[:, HEAD_DIM:] = blocked.astype(qa_ref.dtype)
    kb = k_ref[pl.ds(pl.multiple_of(i * tq, tq), tq), :]
    ka_ref[:, :HEAD_DIM] = kb.astype(ka_ref.dtype)
    ka_ref[:, HEAD_DIM:] = jnp.where(lane == own, NEG_INF, 0.0).astype(ka_ref.dtype)


def _moba_route(mqk, tq=512):
    s = mqk.shape[0]
    aug = 2 * HEAD_DIM
    out = jax.ShapeDtypeStruct((s, MOBA_HEADS * aug), jnp.bfloat16)
    return pl.pallas_call(
        functools.partial(_moba_route_kernel, tq=tq),
        grid=(MOBA_HEADS, s // tq),
        in_specs=[pl.BlockSpec((tq, HEAD_DIM), lambda h, i: (i, h)),
                  pl.BlockSpec((s, HEAD_DIM), lambda h, i: (0, MOBA_HEADS + h))],
        out_specs=[pl.BlockSpec((tq, aug), lambda h, i: (i, h)),
                   pl.BlockSpec((tq, aug), lambda h, i: (i, h))],
        out_shape=[out, out],
        scratch_shapes=[pltpu.VMEM((LANES, HEAD_DIM), jnp.float32)],
        compiler_params=_params(("arbitrary", "arbitrary"),
                                (_nbytes((tq, HEAD_DIM), jnp.float32),
                                 _nbytes((s, HEAD_DIM), jnp.float32),
                                 2 * _nbytes((tq, aug), jnp.bfloat16))),
        name="moba_route",
    )(mqk, mqk)


def _t5_bucket(dist, xp):
    n = xp.maximum(dist, 0)
    max_exact = N_BUCKETS // 2
    nf = xp.maximum(n, 1).astype(xp.float32)
    large = max_exact + (xp.log(nf / max_exact) / math.log(MAX_DISTANCE / max_exact)
                         * (N_BUCKETS - max_exact)).astype(xp.int32)
    large = xp.minimum(large, N_BUCKETS - 1)
    return xp.where(n < max_exact, n, large)


def _toeplitz(g, t):
    heads = g.shape[0]
    rows = jnp.tile(g, (1, t))[:, :t * (2 * t - 1)].reshape(heads, t, 2 * t - 1)
    return rows[:, :, :t]


def _bias_tiles(rel_bias, t):
    assert int(_t5_bucket(np.array([t + 1]), np)[0]) == N_BUCKETS - 1
    tab = rel_bias.T.astype(jnp.float32)
    rel = (tab - tab[:, N_BUCKETS - 1:]) * LOG2E
    c = jnp.arange(2 * t, dtype=jnp.int32)
    d0 = jnp.where(c == 0, 0, 2 * t - c)
    g0 = jnp.where((c == 0) | (c > t), rel[:, _t5_bucket(d0, jnp)], NEG_INF)
    d1 = jnp.where(c < t, t - c, 3 * t - c)
    g1 = rel[:, _t5_bucket(d1, jnp)]
    return _toeplitz(g0, t), _toeplitz(g1, t)


class _Stream(NamedTuple):
    q_ref: Any
    k_ref: Any
    v_ref: Any
    b0_ref: Any
    b1_ref: Any
    s_ref: Any
    p_ref: Any
    m_ref: Any
    a_ref: Any
    l_ref: Any
    acc_ref: Any


_STREAM_SCRATCH = ("s_ref", "p_ref", "m_ref", "a_ref", "l_ref", "acc_ref")


def _stream_scratch(t, dv):
    return [pltpu.VMEM((t, t), jnp.float32), pltpu.VMEM((t, t), jnp.bfloat16),
            pltpu.VMEM((t, LANES), jnp.float32), pltpu.VMEM((t, LANES), jnp.float32),
            pltpu.VMEM((t, LANES), jnp.float32), pltpu.VMEM((t, dv), jnp.float32)]


def _stream_scratch_bytes(t, dv):
    return (_nbytes((t, t), jnp.float32) + _nbytes((t, t), jnp.bfloat16)
            + 3 * _nbytes((t, LANES), jnp.float32) + _nbytes((t, dv), jnp.float32))


def _attn_step(streams, j, tile, t):
    off = pl.multiple_of(j * t, t)
    n_blocks = t // BIAS_TILE
    for st in streams:
        st.s_ref[...] = lax.dot_general(st.q_ref[...], st.k_ref[pl.ds(off, t), :], _NT,
                                        preferred_element_type=jnp.float32)
    for st in streams:
        for r in range(t // ATTN_ROW_CHUNK):
            rows = slice(r * ATTN_ROW_CHUNK, (r + 1) * ATTN_ROW_CHUNK)
            qb, in_block = divmod(r * ATTN_ROW_CHUNK, BIAS_TILE)
            bias_rows = slice(in_block, in_block + ATTN_ROW_CHUNK)
            blocks = {}
            for c in range(n_blocks):
                gap = {"far": 2, "sub": qb + n_blocks - c, "diag": qb - c}[tile]
                if gap < 0:
                    continue
                blk = st.s_ref[rows, c * BIAS_TILE:(c + 1) * BIAS_TILE]
                if gap == 0:
                    blk = blk + st.b0_ref[bias_rows, :]
                elif gap == 1:
                    blk = blk + st.b1_ref[bias_rows, :]
                blocks[c] = blk
            mx = functools.reduce(jnp.maximum, blocks.values())
            m_prev = st.m_ref[rows, :]
            m_new = jnp.maximum(m_prev, jnp.max(mx, axis=1, keepdims=True))
            alpha = jnp.exp2(m_prev - m_new)
            probs = {c: jnp.exp2(blk - m_new) for c, blk in blocks.items()}
            st.m_ref[rows, :] = m_new
            st.a_ref[rows, :] = alpha
            st.l_ref[rows, :] = alpha * st.l_ref[rows, :] + functools.reduce(jnp.add, probs.values())
            for c in range(n_blocks):
                p = probs[c].astype(st.p_ref.dtype) if c in probs else jnp.zeros(
                    (ATTN_ROW_CHUNK, BIAS_TILE), st.p_ref.dtype)
                st.p_ref[rows, c * BIAS_TILE:(c + 1) * BIAS_TILE] = p
    for st in streams:
        pv = jnp.dot(st.p_ref[...], st.v_ref[pl.ds(off, t), :], preferred_element_type=jnp.float32)
        for c in range(pv.shape[1] // LANES):
            cols = slice(c * LANES, (c + 1) * LANES)
            st.acc_ref[:, cols] = st.a_ref[...] * st.acc_ref[:, cols] + pv[:, cols]


def _attend(streams, i, t):
    for st in streams:
        st.m_ref[...] = jnp.full_like(st.m_ref, -jnp.inf)
        st.l_ref[...] = jnp.zeros_like(st.l_ref)
        st.acc_ref[...] = jnp.zeros_like(st.acc_ref)

    def far(j, carry):
        _attn_step(streams, j, "far", t)
        return carry

    lax.fori_loop(0, jnp.maximum(i - 1, 0), far, 0)

    @pl.when(i > 0)
    def _():
        _attn_step(streams, i - 1, "sub", t)

    _attn_step(streams, i, "diag", t)
    return [st.acc_ref[...] / jnp.sum(st.l_ref[...], axis=1, keepdims=True) for st in streams]


def _moba_attn_kernel(qa_ref, qb_ref, ka_ref, kb_ref, va_ref, vb_ref, b0a_ref, b0b_ref, b1a_ref, b1b_ref,
                      o_ref, *scratch, t):
    n = len(_STREAM_SCRATCH)
    streams = [_Stream(qa_ref, ka_ref, va_ref, b0a_ref, b1a_ref, *scratch[:n]),
               _Stream(qb_ref, kb_ref, vb_ref, b0b_ref, b1b_ref, *scratch[n:])]
    oa, ob = _attend(streams, pl.program_id(1), t)
    o_ref[:, :HEAD_DIM] = oa.astype(o_ref.dtype)
    o_ref[:, HEAD_DIM:] = ob.astype(o_ref.dtype)


def _moba_attention(qa, ka, v, v_col0, b0, b1, t=ATTN_TILE):
    s = qa.shape[0]
    aug = 2 * HEAD_DIM

    def pair(spec_of_head):
        return [spec_of_head(0), spec_of_head(1)]

    return pl.pallas_call(
        functools.partial(_moba_attn_kernel, t=t),
        grid=(MOBA_HEADS // 2, s // t),
        in_specs=(pair(lambda d: pl.BlockSpec((t, aug), lambda g, i: (i, 2 * g + d)))
                  + pair(lambda d: pl.BlockSpec((s, aug), lambda g, i: (0, 2 * g + d)))
                  + pair(lambda d: pl.BlockSpec((s, HEAD_DIM), lambda g, i: (0, v_col0 + 2 * g + d)))
                  + pair(lambda d: pl.BlockSpec((None, BIAS_TILE, BIAS_TILE), lambda g, i: (2 * g + d, 0, 0)))
                  + pair(lambda d: pl.BlockSpec((None, BIAS_TILE, BIAS_TILE), lambda g, i: (2 * g + d, 0, 0)))),
        out_specs=pl.BlockSpec((t, 2 * HEAD_DIM), lambda g, i: (i, g)),
        out_shape=jax.ShapeDtypeStruct((s, MOBA_WIDTH), jnp.bfloat16),
        scratch_shapes=2 * _stream_scratch(t, HEAD_DIM),
        compiler_params=_params(("arbitrary", "arbitrary"),
                                (2 * _nbytes((t, aug), jnp.bfloat16), 2 * _nbytes((s, aug), jnp.bfloat16),
                                 2 * _nbytes((s, HEAD_DIM), jnp.bfloat16),
                                 4 * _nbytes((BIAS_TILE, BIAS_TILE), jnp.float32),
                                 _nbytes((t, 2 * HEAD_DIM), jnp.bfloat16)),
                                (2 * _stream_scratch_bytes(t, HEAD_DIM),)),
        name="moba_attention",
    )(qa, qa, ka, ka, v, v, b0, b0, b1, b1)


def _diff_attn_kernel(lam_ref, q1_ref, q2_ref, k1_ref, k2_ref, v_ref, b0_ref, b1_ref, g_ref, o_ref,
                      *scratch, t, out_scale):
    n = len(_STREAM_SCRATCH)
    streams = [_Stream(q1_ref, k1_ref, v_ref, b0_ref, b1_ref, *scratch[:n]),
               _Stream(q2_ref, k2_ref, v_ref, b0_ref, b1_ref, *scratch[n:])]
    o1, o2 = _attend(streams, pl.program_id(1), t)
    y = o1 - lam_ref[0, 0] * o2
    y = y * lax.rsqrt(jnp.mean(y * y, axis=1, keepdims=True) + LN_EPS) * g_ref[...] * out_scale
    o_ref[...] = y.astype(o_ref.dtype)


def _diff_attention(lam, proj, q_col0, k_col0, v_col0, b0, b1, subln_g, out_scale, t=ATTN_TILE):
    s = proj.shape[0]
    return pl.pallas_call(
        functools.partial(_diff_attn_kernel, t=t, out_scale=out_scale),
        grid=(DIFF_HEADS, s // t),
        in_specs=[pl.BlockSpec(memory_space=pltpu.SMEM),
                  pl.BlockSpec((t, HEAD_DIM), lambda h, i: (i, q_col0 + 2 * h)),
                  pl.BlockSpec((t, HEAD_DIM), lambda h, i: (i, q_col0 + 2 * h + 1)),
                  pl.BlockSpec((s, HEAD_DIM), lambda h, i: (0, k_col0 + 2 * h)),
                  pl.BlockSpec((s, HEAD_DIM), lambda h, i: (0, k_col0 + 2 * h + 1)),
                  pl.BlockSpec((s, DIFF_V_DIM), lambda h, i: (0, v_col0 + h)),
                  pl.BlockSpec((None, BIAS_TILE, BIAS_TILE), lambda h, i: (MOBA_HEADS + h, 0, 0)),
                  pl.BlockSpec((None, BIAS_TILE, BIAS_TILE), lambda h, i: (MOBA_HEADS + h, 0, 0)),
                  pl.BlockSpec((1, DIFF_V_DIM), lambda h, i: (0, 0))],
        out_specs=pl.BlockSpec((t, DIFF_V_DIM), lambda h, i: (i, h)),
        out_shape=jax.ShapeDtypeStruct((s, DIFF_WIDTH), jnp.bfloat16),
        scratch_shapes=2 * _stream_scratch(t, DIFF_V_DIM),
        compiler_params=_params(("arbitrary", "arbitrary"),
                                (2 * _nbytes((t, HEAD_DIM), jnp.bfloat16),
                                 2 * _nbytes((s, HEAD_DIM), jnp.bfloat16),
                                 _nbytes((s, DIFF_V_DIM), jnp.bfloat16),
                                 2 * _nbytes((BIAS_TILE, BIAS_TILE), jnp.float32),
                                 _nbytes((t, DIFF_V_DIM), jnp.bfloat16)),
                                (2 * _stream_scratch_bytes(t, DIFF_V_DIM),)),
        name="diff_attention",
    )(lam, proj, proj, proj, proj, proj, b0, b1, subln_g)


def _sigmoid(x):
    return 1.0 / (1.0 + jnp.exp(-x))


def _merge_kernel(ya_ref, yb_ref, wa_ref, wb_ref, ga_ref, gb_ref, ba_ref, bb_ref, o_ref):
    pa = lax.dot_general(ya_ref[...], wa_ref[...], _NN, preferred_element_type=jnp.float32)
    pb = lax.dot_general(yb_ref[...], wb_ref[...], _NN, preferred_element_type=jnp.float32)
    merged = (_sigmoid(ga_ref[...] + ba_ref[...]) * pa + _sigmoid(gb_ref[...] + bb_ref[...]) * pb)
    o_ref[...] = merged.astype(o_ref.dtype)


def _gated_merge(ya, yb, wa, wb, gates, gate_b, tm=512, tn=512):
    s, k = ya.shape
    n = wa.shape[1]
    nb = n // tn
    return pl.pallas_call(
        _merge_kernel,
        grid=(nb, s // tm),
        in_specs=[pl.BlockSpec((tm, k), lambda j, i: (i, 0)),
                  pl.BlockSpec((tm, k), lambda j, i: (i, 0)),
                  pl.BlockSpec((k, tn), lambda j, i: (0, j)),
                  pl.BlockSpec((k, tn), lambda j, i: (0, j)),
                  pl.BlockSpec((tm, tn), lambda j, i: (i, j)),
                  pl.BlockSpec((tm, tn), lambda j, i: (i, nb + j)),
                  pl.BlockSpec((1, tn), lambda j, i: (0, j)),
                  pl.BlockSpec((1, tn), lambda j, i: (0, nb + j))],
        out_specs=pl.BlockSpec((tm, tn), lambda j, i: (i, j)),
        out_shape=jax.ShapeDtypeStruct((s, n), jnp.bfloat16),
        compiler_params=_params(("arbitrary", "arbitrary"),
                                (2 * _nbytes((tm, k), ya.dtype), 2 * _nbytes((k, tn), wa.dtype),
                                 2 * _nbytes((tm, tn), gates.dtype),
                                 _nbytes((tm, tn), jnp.bfloat16))),
        name="gated_merge",
    )(ya, yb, wa, wb, gates, gates, gate_b, gate_b)


def _residual_ln_kernel(x_ref, h_ref, g_ref, b_ref, *o_refs):
    z = DEEPNORM_ALPHA * x_ref[...] + h_ref[...]
    mu = jnp.mean(z, axis=1, keepdims=True)
    zc = z - mu
    var = jnp.mean(zc * zc, axis=1, keepdims=True)
    y = zc * lax.rsqrt(var + LN_EPS) * g_ref[...] + b_ref[...]
    for o_ref in o_refs:
        o_ref[...] = y.astype(o_ref.dtype)


def _residual_ln(x, h, g, b, out_dtypes, tm=256):
    s, d = x.shape
    row = pl.BlockSpec((tm, d), lambda i: (i, 0))
    vec = pl.BlockSpec((1, d), lambda i: (0, 0))
    return pl.pallas_call(
        _residual_ln_kernel,
        grid=(s // tm,),
        in_specs=[row, row, vec, vec],
        out_specs=[row for _ in out_dtypes],
        out_shape=[jax.ShapeDtypeStruct((s, d), dt) for dt in out_dtypes],
        compiler_params=_params(("arbitrary",),
                                [2 * _nbytes((tm, d), jnp.float32)]
                                + [_nbytes((tm, d), dt) for dt in out_dtypes]),
        name="residual_layernorm",
    )(x, h, g, b)


def _top_rows_one_by_one(s, k):
    n = s.shape[0]
    ridx = lax.broadcasted_iota(jnp.int32, s.shape, 0).astype(jnp.float32)
    rows, rank = [], jnp.full(s.shape, float(k), jnp.float32)
    for i in range(k):
        mx = jnp.max(s, axis=0, keepdims=True)
        rows.append(mx)
        first = jnp.min(jnp.where(s == mx, ridx, float(n)), axis=0, keepdims=True)
        hit = ridx == first
        rank = jnp.where(hit, float(i), rank)
        s = jnp.where(hit, -jnp.inf, s)
    return tuple(rows) + (rank,)


def _top_rows(s, k):
    rows, rest, rank = [], s, jnp.full(s.shape, float(k), jnp.float32)
    for i in range(k):
        mx = jnp.max(rest, axis=0, keepdims=True)
        rows.append(mx)
        hit = rest == mx
        rank = jnp.where(hit, float(i), rank)
        rest = jnp.where(hit, -jnp.inf, rest)
    removed = jnp.sum(jnp.where(rest == -jnp.inf, 1.0, 0.0), axis=0, keepdims=True)
    out = lax.cond(jnp.max(removed) > k, lambda: _top_rows_one_by_one(s, k),
                   lambda: tuple(rows) + (rank,))
    return out[:k], out[k]


def _stack_rows(rows):
    k, tm = len(rows), rows[0].shape[1]
    ridx = lax.broadcasted_iota(jnp.int32, (k, tm), 0)
    out = jnp.zeros((k, tm), rows[0].dtype)
    for r, row in enumerate(rows):
        out = jnp.where(ridx == r, row, out)
    return out


def _peer_route_kernel(q_ref, k1_ref, k2_ref, c1_ref, e1_ref, r2_ref, e2_ref, *, tm):
    q = q_ref[...]
    s1 = lax.dot_general(k1_ref[...], q[:, :PEER_HALF], _NT, precision=lax.Precision.HIGHEST,
                         preferred_element_type=jnp.float32)
    s2 = lax.dot_general(k2_ref[...], q[:, PEER_HALF:], _NT, precision=lax.Precision.HIGHEST,
                         preferred_element_type=jnp.float32)
    v1, rank1 = _top_rows(s1, PEER_TOPK)
    v2, rank2 = _top_rows(s2, PEER_TOPK)
    v1m = _stack_rows(v1)
    v2m = _stack_rows(v2)
    half = PEER_TOPK // 2
    assert all((a + 1) * (b + 1) > PEER_TOPK for a in range(1, half) for b in range(half, PEER_TOPK))
    assert all((a + 1) * (b + 1) > PEER_TOPK for a in range(half, PEER_TOPK) for b in range(1, PEER_TOPK))
    cand = jnp.concatenate([v1[0] + v2m] + [v1[a] + v2m[:half] for a in range(1, half)]
                           + [v1m[half:] + v2[0]], axis=0)
    top, _ = _top_rows(cand, PEER_TOPK)
    thr = top[PEER_TOPK - 1]
    z = jnp.sum(jnp.where(cand >= thr, jnp.exp(cand - top[0]), 0.0), axis=0, keepdims=True)
    c1 = jnp.zeros_like(s1)
    for a in range(PEER_TOPK):
        count = jnp.sum(jnp.where(v1[a] + v2m >= thr, 1.0, 0.0), axis=0, keepdims=True)
        c1 = jnp.where(rank1 == float(a), count, c1)
    c1_ref[...] = c1
    e1_ref[...] = jnp.exp(s1 - v1[0])
    r2_ref[...] = rank2.astype(r2_ref.dtype)
    e2_ref[...] = (jnp.exp(s2 - v2[0]) / z).astype(e2_ref.dtype)


def _peer_route(qp, keys1, keys2, tm=512):
    s = qp.shape[0]
    shape = (PEER_HEADS, PEER_N_KEYS, s)
    spec = pl.BlockSpec((None, PEER_N_KEYS, tm), lambda i, h: (h, 0, i))
    key_spec = pl.BlockSpec((PEER_N_KEYS, PEER_HALF), lambda i, h: (0, 0))
    return pl.pallas_call(
        functools.partial(_peer_route_kernel, tm=tm),
        grid=(s // tm, PEER_HEADS),
        in_specs=[pl.BlockSpec((tm, PEER_QUERY_DIM), lambda i, h: (i, h)), key_spec, key_spec],
        out_specs=[spec, spec, spec, spec],
        out_shape=[jax.ShapeDtypeStruct(shape, jnp.float32), jax.ShapeDtypeStruct(shape, jnp.float32),
                   jax.ShapeDtypeStruct(shape, jnp.bfloat16), jax.ShapeDtypeStruct(shape, jnp.bfloat16)],
        compiler_params=_params(("arbitrary", "arbitrary"),
                                (_nbytes((tm, PEER_QUERY_DIM), jnp.float32),
                                 2 * _nbytes((PEER_N_KEYS, PEER_HALF), jnp.float32),
                                 3 * _nbytes((PEER_N_KEYS, tm), jnp.float32))),
        name="peer_route",
    )(qp, keys1, keys2)


def _gelu(x):
    return 0.5 * x * (1.0 + lax.erf(x * math.sqrt(0.5)))


def _peer_expert_kernel(x_ref, u_ref, v_ref, c1_ref, e1_ref, r2_ref, e2_ref, o_ref, *w_refs, tm, te):
    e = pl.program_id(1)

    @pl.when(e == 0)
    def _():
        o_ref[...] = jnp.zeros_like(o_ref)

    gate_dtype = r2_ref.dtype
    groups = MXU_DIM // PEER_N_KEYS
    outs = []
    for chain, w_ref in enumerate(w_refs):
        base = chain * MXU_DIM
        act = lax.dot_general(u_ref[base:base + MXU_DIM, :], x_ref[...], _NT,
                              preferred_element_type=jnp.float32)
        act = _gelu(act).astype(gate_dtype)
        for c in range(groups):
            i1 = e * (te // PEER_N_KEYS) + chain * groups + c
            g = None
            for h in range(PEER_HEADS):
                c1_row = c1_ref[h, pl.ds(i1, 1), :].astype(gate_dtype)
                e1_row = e1_ref[h, pl.ds(i1, 1), :].astype(gate_dtype)
                term = jnp.where(r2_ref[h] < c1_row, e2_ref[h], 0.0) * e1_row
                g = term if g is None else g + term
            lo = c * PEER_N_KEYS
            w_ref[lo:lo + PEER_N_KEYS, :] = g * act[lo:lo + PEER_N_KEYS, :]
        outs.append(lax.dot_general(w_ref[...], v_ref[base:base + MXU_DIM, :], _TN,
                                    preferred_element_type=jnp.float32))
    o_ref[...] += functools.reduce(jnp.add, outs)


def _peer_experts(x, u, v, c1, e1, r2, e2, tm=512, te=512):
    s, d = x.shape
    n_exp = u.shape[0]
    once = pl.Buffered(1)
    gate_spec = pl.BlockSpec((PEER_HEADS, PEER_N_KEYS, tm), lambda i, e: (0, 0, i), pipeline_mode=once)
    gate_bytes = sum(_nbytes((PEER_HEADS, PEER_N_KEYS, tm), a.dtype) for a in (c1, e1, r2, e2))
    return pl.pallas_call(
        functools.partial(_peer_expert_kernel, tm=tm, te=te),
        grid=(s // tm, n_exp // te),
        in_specs=[pl.BlockSpec((tm, d), lambda i, e: (i, 0), pipeline_mode=once),
                  pl.BlockSpec((te, d), lambda i, e: (e, 0)),
                  pl.BlockSpec((te, d), lambda i, e: (e, 0)),
                  gate_spec, gate_spec, gate_spec, gate_spec],
        out_specs=pl.BlockSpec((tm, d), lambda i, e: (i, 0)),
        out_shape=jax.ShapeDtypeStruct((s, d), jnp.float32),
        scratch_shapes=[pltpu.VMEM((MXU_DIM, tm), r2.dtype) for _ in range(te // MXU_DIM)],
        compiler_params=_params(("arbitrary", "arbitrary"),
                                (2 * _nbytes((te, d), u.dtype), _nbytes((tm, d), jnp.float32)),
                                (_nbytes((tm, d), x.dtype), gate_bytes, _nbytes((te, tm), r2.dtype))),
        name="peer_experts",
    )(x, u, v, c1, e1, r2, e2)


def _layer(x, layer, w_in, gate_b, rel_bias, lq1, lk1, lq2, lk2, subln_g, w_o_moba, w_o_diff, w_out,
           ln1_g, ln1_b, peer_wq, keys1, keys2, peer_u, peer_v, ln2_g, ln2_b):
    bf16 = jnp.bfloat16
    f32 = jnp.float32
    lambda_init = 0.8 - 0.6 * math.exp(-0.3 * layer)
    xb = x.astype(bf16)

    c0 = 2 * MOBA_WIDTH
    c1 = c0 + MOBA_WIDTH + 3 * DIFF_WIDTH
    col_scale = jnp.concatenate([jnp.ones((MOBA_WIDTH,), f32), jnp.full((DIFF_WIDTH,), QK_SCALE_LOG2, f32),
                                 jnp.ones((2 * DIFF_WIDTH,), f32)]).reshape(1, c1 - c0)
    mqk = _matmul(xb, w_in, 0, c0, f32, 512, 1024, "proj_moba_qk")
    mid = _matmul(xb, w_in, c0, c1 - c0, bf16, 512, 1024, "proj_attn", col_scale)
    gates = _matmul(xb, w_in, c1, 2 * D_MODEL, f32, 512, 1024, "proj_gates")

    b0, b1 = _bias_tiles(rel_bias, BIAS_TILE)
    qa, ka = _moba_route(mqk)
    y_a = _moba_attention(qa, ka, mid, 0, b0, b1)

    lam = (jnp.exp(jnp.sum(lq1.astype(f32) * lk1.astype(f32)))
           - jnp.exp(jnp.sum(lq2.astype(f32) * lk2.astype(f32))) + lambda_init).reshape(1, 1)
    hb = MOBA_WIDTH // HEAD_DIM
    y_b = _diff_attention(lam, mid, hb, hb + DIFF_WIDTH // HEAD_DIM,
                          (MOBA_WIDTH + 2 * DIFF_WIDTH) // DIFF_V_DIM, b0, b1,
                          subln_g.reshape(1, DIFF_V_DIM).astype(f32), 1.0 - lambda_init)

    merged = _gated_merge(y_a, y_b, w_o_moba, w_o_diff, gates,
                          gate_b.reshape(1, 2 * D_MODEL).astype(f32))
    h = _matmul(merged, w_out, 0, D_MODEL, f32, 512, 1024, "proj_out")
    x1, x1b = _residual_ln(x, h, ln1_g.reshape(1, D_MODEL), ln1_b.reshape(1, D_MODEL), (f32, bf16))

    qp = _matmul(x1b, peer_wq, 0, PEER_HEADS * PEER_QUERY_DIM, f32, 512, 1024, "peer_query")
    c1, e1, r2, e2 = _peer_route(qp, keys1, keys2)
    h2 = _peer_experts(x1b, peer_u.astype(bf16), peer_v.astype(bf16), c1, e1, r2, e2)
    (x2,) = _residual_ln(x1, h2, ln2_g.reshape(1, D_MODEL), ln2_b.reshape(1, D_MODEL), (f32,))
    return x2


def kernel(x, w_in, gate_b, rel_bias, lambda_q1, lambda_k1, lambda_q2, lambda_k2, subln_g, w_o_moba,
           w_o_diff, w_out, ln1_g, ln1_b, peer_wq, peer_keys1, peer_keys2, peer_u, peer_v, ln2_g, ln2_b):
    b, s, d = x.shape
    assert (b, s, d) == (1, SEQ, D_MODEL) and w_in.shape[0] == DEPTH
    xs = x.reshape(s, d)
    for layer in range(DEPTH):
        xs = _layer(xs, layer, w_in[layer], gate_b[layer], rel_bias, lambda_q1[layer], lambda_k1[layer],
                    lambda_q2[layer], lambda_k2[layer], subln_g[layer], w_o_moba[layer], w_o_diff[layer],
                    w_out[layer], ln1_g[layer], ln1_b[layer], peer_wq[layer], peer_keys1[layer],
                    peer_keys2[layer], peer_u[layer], peer_v[layer], ln2_g[layer], ln2_b[layer])
    return xs.reshape(b, s, d)
```

```python
import functools
import math
from typing import Any, NamedTuple

import jax
import jax.numpy as jnp
import numpy as np
from jax import lax
from jax.experimental import pallas as pl
from jax.experimental.pallas import tpu as pltpu

D_MODEL = 4096
SEQ = 8192
DEPTH = 1
HEAD_DIM = 128
MOBA_HEADS = 16
MOBA_WIDTH = MOBA_HEADS * HEAD_DIM
MOBA_BLOCK = 256
MOBA_TOPK = 3
N_MOBA_BLOCKS = SEQ // MOBA_BLOCK
DIFF_HEADS = 8
DIFF_V_DIM = 2 * HEAD_DIM
DIFF_WIDTH = DIFF_HEADS * DIFF_V_DIM
N_BUCKETS = 32
MAX_DISTANCE = 128
PEER_HEADS = 8
PEER_N_KEYS = 128
PEER_HALF = 128
PEER_QUERY_DIM = 2 * PEER_HALF
PEER_TOPK = 16
DEEPNORM_ALPHA = (2.0 * DEPTH) ** 0.25
LN_EPS = 1e-5
NEG_INF = -1e30

LOG2E = math.log2(math.e)
QK_SCALE_LOG2 = HEAD_DIM ** -0.5 * LOG2E

LANES = 128
MXU_DIM = 256
VMEM_LIMIT_CAP = 60000 * 1024
COMPILER_SCRATCH_BYTES = 8 << 20

ATTN_TILE = 512
BIAS_TILE = LANES
ATTN_ROW_CHUNK = 32

_NT = (((1,), (1,)), ((), ()))
_TN = (((0,), (0,)), ((), ()))


def _nbytes(shape, dtype):
    return int(np.prod(shape)) * jnp.dtype(dtype).itemsize


def _params(semantics, pipelined=(), resident=()):
    need = 2 * sum(pipelined) + sum(resident) + COMPILER_SCRATCH_BYTES
    return pltpu.CompilerParams(dimension_semantics=semantics,
                                vmem_limit_bytes=min(need, VMEM_LIMIT_CAP))


_NN = (((1,), (0,)), ((), ()))


def _matmul_kernel(a_ref, b_ref, *rest):
    *scale, o_ref = rest
    acc = lax.dot_general(a_ref[...], b_ref[...], _NN, preferred_element_type=jnp.float32)
    for s_ref in scale:
        acc = acc * s_ref[...]
    o_ref[...] = acc.astype(o_ref.dtype)


def _matmul(a, w, col0, n, out_dtype, tm, tn, name, col_scale=None):
    m, k = a.shape
    c0 = col0 // tn
    scale_specs = [] if col_scale is None else [pl.BlockSpec((1, tn), lambda j, i: (0, j))]
    scale_args = [] if col_scale is None else [col_scale]
    return pl.pallas_call(
        _matmul_kernel,
        grid=(n // tn, m // tm),
        in_specs=[pl.BlockSpec((tm, k), lambda j, i: (i, 0)),
                  pl.BlockSpec((k, tn), lambda j, i: (0, c0 + j))] + scale_specs,
        out_specs=pl.BlockSpec((tm, tn), lambda j, i: (i, j)),
        out_shape=jax.ShapeDtypeStruct((m, n), out_dtype),
        compiler_params=_params(("arbitrary", "arbitrary"),
                                (_nbytes((tm, k), a.dtype), _nbytes((k, tn), w.dtype),
                                 _nbytes((tm, tn), out_dtype))),
        name=name,
    )(a, w, *scale_args)


def _moba_route_kernel(q_ref, k_ref, qa_ref, ka_ref, kmean_ref, *, tq):
    i = pl.program_id(1)

    @pl.when(i == 0)
    def _():
        k3 = k_ref[...].reshape(N_MOBA_BLOCKS, MOBA_BLOCK, HEAD_DIM)
        kmean_ref[...] = jnp.zeros_like(kmean_ref)
        kmean_ref[0:N_MOBA_BLOCKS, :] = jnp.sum(k3, axis=1) * (1.0 / MOBA_BLOCK)

    q = q_ref[...]
    gate = lax.dot_general(q, kmean_ref[...], _NT, precision=lax.Precision.HIGHEST,
                           preferred_element_type=jnp.float32)
    lane = lax.broadcasted_iota(jnp.int32, (tq, LANES), 1)
    row = lax.broadcasted_iota(jnp.int32, (tq, LANES), 0)
    own = jnp.right_shift(i * tq + row, int(math.log2(MOBA_BLOCK)))
    lane_f = lane.astype(jnp.float32)
    past = lane < own
    g = jnp.where(past, gate, NEG_INF)
    picked = jnp.zeros((tq, LANES), jnp.float32)
    for _ in range(MOBA_TOPK):
        mx = jnp.max(g, axis=1, keepdims=True)
        first = jnp.min(jnp.where(g == mx, lane_f, float(LANES)), axis=1, keepdims=True)
        hit = lane_f == first
        picked = jnp.where(hit, 1.0, picked)
        g = jnp.where(hit, -jnp.inf, g)
    allowed = jnp.where(past, picked, 0.0)
    allowed = jnp.where(lane == own, 1.0, allowed)
    blocked = jnp.where(lane < N_MOBA_BLOCKS, 1.0 - allowed, 0.0)
    qa_ref[:, :HEAD_DIM] = (q * QK_SCALE_LOG2).astype(qa_ref.dtype)
    qa_ref[:, HEAD_DIM:] = blocked.astype(qa_ref.dtype)
    kb = k_ref[pl.ds(pl.multiple_of(i * tq, tq), tq), :]
    ka_ref[:, :HEAD_DIM] = kb.astype(ka_ref.dtype)
    ka_ref[:, HEAD_DIM:] = jnp.where(lane == own, NEG_INF, 0.0).astype(ka_ref.dtype)


def _moba_route(mqk, tq=1024):
    s = mqk.shape[0]
    aug = 2 * HEAD_DIM
    out = jax.ShapeDtypeStruct((s, MOBA_HEADS * aug), jnp.bfloat16)
    return pl.pallas_call(
        functools.partial(_moba_route_kernel, tq=tq),
        grid=(MOBA_HEADS, s // tq),
        in_specs=[pl.BlockSpec((tq, HEAD_DIM), lambda h, i: (i, h)),
                  pl.BlockSpec((s, HEAD_DIM), lambda h, i: (0, MOBA_HEADS + h))],
        out_specs=[pl.BlockSpec((tq, aug), lambda h, i: (i, h)),
                   pl.BlockSpec((tq, aug), lambda h, i: (i, h))],
        out_shape=[out, out],
        scratch_shapes=[pltpu.VMEM((LANES, HEAD_DIM), jnp.float32)],
        compiler_params=_params(("arbitrary", "arbitrary"),
                                (_nbytes((tq, HEAD_DIM), jnp.float32),
                                 _nbytes((s, HEAD_DIM), jnp.float32),
                                 2 * _nbytes((tq, aug), jnp.bfloat16))),
        name="moba_route",
    )(mqk, mqk)


def _t5_bucket(dist, xp):
    n = xp.maximum(dist, 0)
    max_exact = N_BUCKETS // 2
    nf = xp.maximum(n, 1).astype(xp.float32)
    large = max_exact + (xp.log(nf / max_exact) / math.log(MAX_DISTANCE / max_exact)
                         * (N_BUCKETS - max_exact)).astype(xp.int32)
    large = xp.minimum(large, N_BUCKETS - 1)
    return xp.where(n < max_exact, n, large)


def _toeplitz(g, t):
    heads = g.shape[0]
    rows = jnp.tile(g, (1, t))[:, :t * (2 * t - 1)].reshape(heads, t, 2 * t - 1)
    return rows[:, :, :t]


def _bias_tiles(rel_bias, t):
    assert int(_t5_bucket(np.array([t + 1]), np)[0]) == N_BUCKETS - 1
    tab = rel_bias.T.astype(jnp.float32)
    rel = (tab - tab[:, N_BUCKETS - 1:]) * LOG2E
    c = jnp.arange(2 * t, dtype=jnp.int32)
    d0 = jnp.where(c == 0, 0, 2 * t - c)
    g0 = jnp.where((c == 0) | (c > t), rel[:, _t5_bucket(d0, jnp)], NEG_INF)
    d1 = jnp.where(c < t, t - c, 3 * t - c)
    g1 = rel[:, _t5_bucket(d1, jnp)]
    return _toeplitz(g0, t), _toeplitz(g1, t)


class _Stream(NamedTuple):
    q_ref: Any
    k_ref: Any
    v_ref: Any
    b0_ref: Any
    b1_ref: Any
    s_ref: Any
    p_ref: Any
    m_ref: Any
    a_ref: Any
    l_ref: Any
    acc_ref: Any


_STREAM_SCRATCH = ("s_ref", "p_ref", "m_ref", "a_ref", "l_ref", "acc_ref")


def _stream_scratch(t, dv):
    return [pltpu.VMEM((t, t), jnp.float32), pltpu.VMEM((t, t), jnp.bfloat16),
            pltpu.VMEM((t, LANES), jnp.float32), pltpu.VMEM((t, LANES), jnp.float32),
            pltpu.VMEM((t, LANES), jnp.float32), pltpu.VMEM((t, dv), jnp.float32)]


def _stream_scratch_bytes(t, dv):
    return (_nbytes((t, t), jnp.float32) + _nbytes((t, t), jnp.bfloat16)
            + 3 * _nbytes((t, LANES), jnp.float32) + _nbytes((t, dv), jnp.float32))


def _attn_step(streams, j, tile, t):
    off = pl.multiple_of(j * t, t)
    n_blocks = t // BIAS_TILE
    for st in streams:
        st.s_ref[...] = lax.dot_general(st.q_ref[...], st.k_ref[pl.ds(off, t), :], _NT,
                                        preferred_element_type=jnp.float32)
    for st in streams:
        for r in range(t // ATTN_ROW_CHUNK):
            rows = slice(r * ATTN_ROW_CHUNK, (r + 1) * ATTN_ROW_CHUNK)
            qb, in_block = divmod(r * ATTN_ROW_CHUNK, BIAS_TILE)
            bias_rows = slice(in_block, in_block + ATTN_ROW_CHUNK)
            blocks = {}
            for c in range(n_blocks):
                gap = {"far": 2, "sub": qb + n_blocks - c, "diag": qb - c}[tile]
                if gap < 0:
                    continue
                blk = st.s_ref[rows, c * BIAS_TILE:(c + 1) * BIAS_TILE]
                if gap == 0:
                    blk = blk + st.b0_ref[bias_rows, :]
                elif gap == 1:
                    blk = blk + st.b1_ref[bias_rows, :]
                blocks[c] = blk
            mx = functools.reduce(jnp.maximum, blocks.values())
            m_prev = st.m_ref[rows, :]
            m_new = jnp.maximum(m_prev, jnp.max(mx, axis=1, keepdims=True))
            alpha = jnp.exp2(m_prev - m_new)
            probs = {c: jnp.exp2(blk - m_new) for c, blk in blocks.items()}
            st.m_ref[rows, :] = m_new
            st.a_ref[rows, :] = alpha
            st.l_ref[rows, :] = alpha * st.l_ref[rows, :] + functools.reduce(jnp.add, probs.values())
            for c in range(n_blocks):
                p = probs[c].astype(st.p_ref.dtype) if c in probs else jnp.zeros(
                    (ATTN_ROW_CHUNK, BIAS_TILE), st.p_ref.dtype)
                st.p_ref[rows, c * BIAS_TILE:(c + 1) * BIAS_TILE] = p
    for st in streams:
        pv = jnp.dot(st.p_ref[...], st.v_ref[pl.ds(off, t), :], preferred_element_type=jnp.float32)
        for c in range(pv.shape[1] // LANES):
            cols = slice(c * LANES, (c + 1) * LANES)
            st.acc_ref[:, cols] = st.a_ref[...] * st.acc_ref[:, cols] + pv[:, cols]


def _attend(streams, i, t):
    for st in streams:
        st.m_ref[...] = jnp.full_like(st.m_ref, -jnp.inf)
        st.l_ref[...] = jnp.zeros_like(st.l_ref)
        st.acc_ref[...] = jnp.zeros_like(st.acc_ref)

    def far(j, carry):
        _attn_step(streams, j, "far", t)
        return carry

    lax.fori_loop(0, jnp.maximum(i - 1, 0), far, 0)

    @pl.when(i > 0)
    def _():
        _attn_step(streams, i - 1, "sub", t)

    _attn_step(streams, i, "diag", t)
    return [st.acc_ref[...] / jnp.sum(st.l_ref[...], axis=1, keepdims=True) for st in streams]


def _moba_attn_kernel(qa_ref, qb_ref, ka_ref, kb_ref, va_ref, vb_ref, b0a_ref, b0b_ref, b1a_ref, b1b_ref,
                      o_ref, *scratch, t):
    n = len(_STREAM_SCRATCH)
    streams = [_Stream(qa_ref, ka_ref, va_ref, b0a_ref, b1a_ref, *scratch[:n]),
               _Stream(qb_ref, kb_ref, vb_ref, b0b_ref, b1b_ref, *scratch[n:])]
    oa, ob = _attend(streams, pl.program_id(1), t)
    o_ref[:, :HEAD_DIM] = oa.astype(o_ref.dtype)
    o_ref[:, HEAD_DIM:] = ob.astype(o_ref.dtype)


def _moba_attention(qa, ka, v, v_col0, b0, b1, t=ATTN_TILE):
    s = qa.shape[0]
    aug = 2 * HEAD_DIM

    def pair(spec_of_head):
        return [spec_of_head(0), spec_of_head(1)]

    return pl.pallas_call(
        functools.partial(_moba_attn_kernel, t=t),
        grid=(MOBA_HEADS // 2, s // t),
        in_specs=(pair(lambda d: pl.BlockSpec((t, aug), lambda g, i: (i, 2 * g + d)))
                  + pair(lambda d: pl.BlockSpec((s, aug), lambda g, i: (0, 2 * g + d)))
                  + pair(lambda d: pl.BlockSpec((s, HEAD_DIM), lambda g, i: (0, v_col0 + 2 * g + d)))
                  + pair(lambda d: pl.BlockSpec((None, BIAS_TILE, BIAS_TILE), lambda g, i: (2 * g + d, 0, 0)))
                  + pair(lambda d: pl.BlockSpec((None, BIAS_TILE, BIAS_TILE), lambda g, i: (2 * g + d, 0, 0)))),
        out_specs=pl.BlockSpec((t, 2 * HEAD_DIM), lambda g, i: (i, g)),
        out_shape=jax.ShapeDtypeStruct((s, MOBA_WIDTH), jnp.bfloat16),
        scratch_shapes=2 * _stream_scratch(t, HEAD_DIM),
        compiler_params=_params(("arbitrary", "arbitrary"),
                                (2 * _nbytes((t, aug), jnp.bfloat16), 2 * _nbytes((s, aug), jnp.bfloat16),
                                 2 * _nbytes((s, HEAD_DIM), jnp.bfloat16),
                                 4 * _nbytes((BIAS_TILE, BIAS_TILE), jnp.float32),
                                 _nbytes((t, 2 * HEAD_DIM), jnp.bfloat16)),
                                (2 * _stream_scratch_bytes(t, HEAD_DIM),)),
        name="moba_attention",
    )(qa, qa, ka, ka, v, v, b0, b0, b1, b1)


def _diff_attn_kernel(lam_ref, q1_ref, q2_ref, k1_ref, k2_ref, v_ref, b0_ref, b1_ref, g_ref, o_ref,
                      *scratch, t, out_scale):
    n = len(_STREAM_SCRATCH)
    streams = [_Stream(q1_ref, k1_ref, v_ref, b0_ref, b1_ref, *scratch[:n]),
               _Stream(q2_ref, k2_ref, v_ref, b0_ref, b1_ref, *scratch[n:])]
    o1, o2 = _attend(streams, pl.program_id(1), t)
    y = o1 - lam_ref[0, 0] * o2
    y = y * lax.rsqrt(jnp.mean(y * y, axis=1, keepdims=True) + LN_EPS) * g_ref[...] * out_scale
    o_ref[...] = y.astype(o_ref.dtype)


def _diff_attention(lam, proj, q_col0, k_col0, v_col0, b0, b1, subln_g, out_scale, t=ATTN_TILE):
    s = proj.shape[0]
    return pl.pallas_call(
        functools.partial(_diff_attn_kernel, t=t, out_scale=out_scale),
        grid=(DIFF_HEADS, s // t),
        in_specs=[pl.BlockSpec(memory_space=pltpu.SMEM),
                  pl.BlockSpec((t, HEAD_DIM), lambda h, i: (i, q_col0 + 2 * h)),
                  pl.BlockSpec((t, HEAD_DIM), lambda h, i: (i, q_col0 + 2 * h + 1)),
                  pl.BlockSpec((s, HEAD_DIM), lambda h, i: (0, k_col0 + 2 * h)),
                  pl.BlockSpec((s, HEAD_DIM), lambda h, i: (0, k_col0 + 2 * h + 1)),
                  pl.BlockSpec((s, DIFF_V_DIM), lambda h, i: (0, v_col0 + h)),
                  pl.BlockSpec((None, BIAS_TILE, BIAS_TILE), lambda h, i: (MOBA_HEADS + h, 0, 0)),
                  pl.BlockSpec((None, BIAS_TILE, BIAS_TILE), lambda h, i: (MOBA_HEADS + h, 0, 0)),
                  pl.BlockSpec((1, DIFF_V_DIM), lambda h, i: (0, 0))],
        out_specs=pl.BlockSpec((t, DIFF_V_DIM), lambda h, i: (i, h)),
        out_shape=jax.ShapeDtypeStruct((s, DIFF_WIDTH), jnp.bfloat16),
        scratch_shapes=2 * _stream_scratch(t, DIFF_V_DIM),
        compiler_params=_params(("arbitrary", "arbitrary"),
                                (2 * _nbytes((t, HEAD_DIM), jnp.bfloat16),
                                 2 * _nbytes((s, HEAD_DIM), jnp.bfloat16),
                                 _nbytes((s, DIFF_V_DIM), jnp.bfloat16),
                                 2 * _nbytes((BIAS_TILE, BIAS_TILE), jnp.float32),
                                 _nbytes((t, DIFF_V_DIM), jnp.bfloat16)),
                                (2 * _stream_scratch_bytes(t, DIFF_V_DIM),)),
        name="diff_attention",
    )(lam, proj, proj, proj, proj, proj, b0, b1, subln_g)


def _sigmoid(x):
    return 1.0 / (1.0 + jnp.exp(-x))


def _merge_kernel(ya_ref, yb_ref, wa_ref, wb_ref, ga_ref, gb_ref, ba_ref, bb_ref, o_ref):
    pa = lax.dot_general(ya_ref[...], wa_ref[...], _NN, preferred_element_type=jnp.float32)
    pb = lax.dot_general(yb_ref[...], wb_ref[...], _NN, preferred_element_type=jnp.float32)
    merged = (_sigmoid(ga_ref[...] + ba_ref[...]) * pa + _sigmoid(gb_ref[...] + bb_ref[...]) * pb)
    o_ref[...] = merged.astype(o_ref.dtype)


def _gated_merge(ya, yb, wa, wb, gates, gate_b, tm=256, tn=1024):
    s, k = ya.shape
    n = wa.shape[1]
    nb = n // tn
    return pl.pallas_call(
        _merge_kernel,
        grid=(nb, s // tm),
        in_specs=[pl.BlockSpec((tm, k), lambda j, i: (i, 0)),
                  pl.BlockSpec((tm, k), lambda j, i: (i, 0)),
                  pl.BlockSpec((k, tn), lambda j, i: (0, j)),
                  pl.BlockSpec((k, tn), lambda j, i: (0, j)),
                  pl.BlockSpec((tm, tn), lambda j, i: (i, j)),
                  pl.BlockSpec((tm, tn), lambda j, i: (i, nb + j)),
                  pl.BlockSpec((1, tn), lambda j, i: (0, j)),
                  pl.BlockSpec((1, tn), lambda j, i: (0, nb + j))],
        out_specs=pl.BlockSpec((tm, tn), lambda j, i: (i, j)),
        out_shape=jax.ShapeDtypeStruct((s, n), jnp.bfloat16),
        compiler_params=_params(("arbitrary", "arbitrary"),
                                (2 * _nbytes((tm, k), ya.dtype), 2 * _nbytes((k, tn), wa.dtype),
                                 2 * _nbytes((tm, tn), gates.dtype),
                                 _nbytes((tm, tn), jnp.bfloat16))),
        name="gated_merge",
    )(ya, yb, wa, wb, gates, gates, gate_b, gate_b)


def _residual_ln_kernel(x_ref, h_ref, g_ref, b_ref, *o_refs):
    z = DEEPNORM_ALPHA * x_ref[...] + h_ref[...]
    mu = jnp.mean(z, axis=1, keepdims=True)
    zc = z - mu
    var = jnp.mean(zc * zc, axis=1, keepdims=True)
    y = zc * lax.rsqrt(var + LN_EPS) * g_ref[...] + b_ref[...]
    for o_ref in o_refs:
        o_ref[...] = y.astype(o_ref.dtype)


def _residual_ln(x, h, g, b, out_dtypes, tm=256):
    s, d = x.shape
    row = pl.BlockSpec((tm, d), lambda i: (i, 0))
    vec = pl.BlockSpec((1, d), lambda i: (0, 0))
    return pl.pallas_call(
        _residual_ln_kernel,
        grid=(s // tm,),
        in_specs=[row, row, vec, vec],
        out_specs=[row for _ in out_dtypes],
        out_shape=[jax.ShapeDtypeStruct((s, d), dt) for dt in out_dtypes],
        compiler_params=_params(("arbitrary",),
                                [2 * _nbytes((tm, d), jnp.float32)]
                                + [_nbytes((tm, d), dt) for dt in out_dtypes]),
        name="residual_layernorm",
    )(x, h, g, b)


def _top_rows_one_by_one(s, k):
    n = s.shape[0]
    ridx = lax.broadcasted_iota(jnp.int32, s.shape, 0).astype(jnp.float32)
    rows, rank = [], jnp.full(s.shape, float(k), jnp.float32)
    for i in range(k):
        mx = jnp.max(s, axis=0, keepdims=True)
        rows.append(mx)
        first = jnp.min(jnp.where(s == mx, ridx, float(n)), axis=0, keepdims=True)
        hit = ridx == first
        rank = jnp.where(hit, float(i), rank)
        s = jnp.where(hit, -jnp.inf, s)
    return tuple(rows) + (rank,)


def _top_rows(s, k):
    rows, rest, rank = [], s, jnp.full(s.shape, float(k), jnp.float32)
    for i in range(k):
        mx = jnp.max(rest, axis=0, keepdims=True)
        rows.append(mx)
        hit = rest == mx
        rank = jnp.where(hit, float(i), rank)
        rest = jnp.where(hit, -jnp.inf, rest)
    removed = jnp.sum(jnp.where(rest == -jnp.inf, 1.0, 0.0), axis=0, keepdims=True)
    out = lax.cond(jnp.max(removed) > k, lambda: _top_rows_one_by_one(s, k),
                   lambda: tuple(rows) + (rank,))
    return out[:k], out[k]


def _stack_rows(rows):
    k, tm = len(rows), rows[0].shape[1]
    ridx = lax.broadcasted_iota(jnp.int32, (k, tm), 0)
    out = jnp.zeros((k, tm), rows[0].dtype)
    for r, row in enumerate(rows):
        out = jnp.where(ridx == r, row, out)
    return out


def _peer_route_kernel(q_ref, k1_ref, k2_ref, c1_ref, e1_ref, r2_ref, e2_ref, *, tm):
    q = q_ref[...]
    s1 = lax.dot_general(k1_ref[...], q[:, :PEER_HALF], _NT, precision=lax.Precision.HIGHEST,
                         preferred_element_type=jnp.float32)
    s2 = lax.dot_general(k2_ref[...], q[:, PEER_HALF:], _NT, precision=lax.Precision.HIGHEST,
                         preferred_element_type=jnp.float32)
    v1, rank1 = _top_rows(s1, PEER_TOPK)
    v2, rank2 = _top_rows(s2, PEER_TOPK)
    v1m = _stack_rows(v1)
    v2m = _stack_rows(v2)
    half = PEER_TOPK // 2
    assert all((a + 1) * (b + 1) > PEER_TOPK for a in range(1, half) for b in range(half, PEER_TOPK))
    assert all((a + 1) * (b + 1) > PEER_TOPK for a in range(half, PEER_TOPK) for b in range(1, PEER_TOPK))
    cand = jnp.concatenate([v1[0] + v2m] + [v1[a] + v2m[:half] for a in range(1, half)]
                           + [v1m[half:] + v2[0]], axis=0)
    top, _ = _top_rows(cand, PEER_TOPK)
    thr = top[PEER_TOPK - 1]
    sums = [v1[a] + v2m for a in range(PEER_TOPK)]
    above = [jnp.sum(jnp.where(sm > thr, 1.0, 0.0), axis=0, keepdims=True) for sm in sums]
    equal = [jnp.sum(jnp.where(sm == thr, 1.0, 0.0), axis=0, keepdims=True) for sm in sums]
    spare = float(PEER_TOPK) - functools.reduce(jnp.add, above)
    bidx = lax.broadcasted_iota(jnp.int32, (PEER_TOPK, tm), 0).astype(jnp.float32)
    c1 = jnp.zeros_like(s1)
    z = jnp.zeros_like(thr)
    for a in range(PEER_TOPK):
        taken = jnp.minimum(equal[a], spare)
        spare = spare - taken
        count = above[a] + taken
        z = z + jnp.sum(jnp.where(bidx < count, jnp.exp(sums[a] - top[0]), 0.0), axis=0, keepdims=True)
        c1 = jnp.where(rank1 == float(a), count, c1)
    c1_ref[...] = c1
    e1_ref[...] = jnp.exp(s1 - v1[0])
    r2_ref[...] = rank2.astype(r2_ref.dtype)
    e2_ref[...] = (jnp.exp(s2 - v2[0]) / z).astype(e2_ref.dtype)


def _peer_route(qp, keys1, keys2, tm=512):
    s = qp.shape[0]
    shape = (PEER_HEADS, PEER_N_KEYS, s)
    spec = pl.BlockSpec((None, PEER_N_KEYS, tm), lambda i, h: (h, 0, i))
    key_spec = pl.BlockSpec((PEER_N_KEYS, PEER_HALF), lambda i, h: (0, 0))
    return pl.pallas_call(
        functools.partial(_peer_route_kernel, tm=tm),
        grid=(s // tm, PEER_HEADS),
        in_specs=[pl.BlockSpec((tm, PEER_QUERY_DIM), lambda i, h: (i, h)), key_spec, key_spec],
        out_specs=[spec, spec, spec, spec],
        out_shape=[jax.ShapeDtypeStruct(shape, jnp.float32), jax.ShapeDtypeStruct(shape, jnp.float32),
                   jax.ShapeDtypeStruct(shape, jnp.bfloat16), jax.ShapeDtypeStruct(shape, jnp.bfloat16)],
        compiler_params=_params(("arbitrary", "arbitrary"),
                                (_nbytes((tm, PEER_QUERY_DIM), jnp.float32),
                                 2 * _nbytes((PEER_N_KEYS, PEER_HALF), jnp.float32),
                                 3 * _nbytes((PEER_N_KEYS, tm), jnp.float32))),
        name="peer_route",
    )(qp, keys1, keys2)


def _gelu(x):
    return 0.5 * x * (1.0 + lax.erf(x * math.sqrt(0.5)))


def _peer_expert_kernel(x_ref, u_ref, v_ref, c1_ref, e1_ref, r2_ref, e2_ref, o_ref, *w_refs, tm, te):
    e = pl.program_id(1)

    @pl.when(e == 0)
    def _():
        o_ref[...] = jnp.zeros_like(o_ref)

    gate_dtype = r2_ref.dtype
    groups = MXU_DIM // PEER_N_KEYS
    outs = []
    for chain, w_ref in enumerate(w_refs):
        base = chain * MXU_DIM
        act = lax.dot_general(u_ref[base:base + MXU_DIM, :], x_ref[...], _NT,
                              preferred_element_type=jnp.float32)
        act = _gelu(act).astype(gate_dtype)
        for c in range(groups):
            i1 = e * (te // PEER_N_KEYS) + chain * groups + c
            g = None
            for h in range(PEER_HEADS):
                c1_row = c1_ref[h, pl.ds(i1, 1), :].astype(gate_dtype)
                e1_row = e1_ref[h, pl.ds(i1, 1), :].astype(gate_dtype)
                term = jnp.where(r2_ref[h] < c1_row, e2_ref[h], 0.0) * e1_row
                g = term if g is None else g + term
            lo = c * PEER_N_KEYS
            w_ref[lo:lo + PEER_N_KEYS, :] = g * act[lo:lo + PEER_N_KEYS, :]
        outs.append(lax.dot_general(w_ref[...], v_ref[base:base + MXU_DIM, :], _TN,
                                    preferred_element_type=jnp.float32))
    o_ref[...] += functools.reduce(jnp.add, outs)


def _peer_experts(x, u, v, c1, e1, r2, e2, tm=512, te=512):
    s, d = x.shape
    n_exp = u.shape[0]
    once = pl.Buffered(1)
    gate_spec = pl.BlockSpec((PEER_HEADS, PEER_N_KEYS, tm), lambda i, e: (0, 0, i), pipeline_mode=once)
    gate_bytes = sum(_nbytes((PEER_HEADS, PEER_N_KEYS, tm), a.dtype) for a in (c1, e1, r2, e2))
    return pl.pallas_call(
        functools.partial(_peer_expert_kernel, tm=tm, te=te),
        grid=(s // tm, n_exp // te),
        in_specs=[pl.BlockSpec((tm, d), lambda i, e: (i, 0), pipeline_mode=once),
                  pl.BlockSpec((te, d), lambda i, e: (e, 0)),
                  pl.BlockSpec((te, d), lambda i, e: (e, 0)),
                  gate_spec, gate_spec, gate_spec, gate_spec],
        out_specs=pl.BlockSpec((tm, d), lambda i, e: (i, 0)),
        out_shape=jax.ShapeDtypeStruct((s, d), jnp.float32),
        scratch_shapes=[pltpu.VMEM((MXU_DIM, tm), r2.dtype) for _ in range(te // MXU_DIM)],
        compiler_params=_params(("arbitrary", "arbitrary"),
                                (2 * _nbytes((te, d), u.dtype), _nbytes((tm, d), jnp.float32)),
                                (_nbytes((tm, d), x.dtype), gate_bytes, _nbytes((te, tm), r2.dtype))),
        name="peer_experts",
    )(x, u, v, c1, e1, r2, e2)


def _layer(x, layer, w_in, gate_b, rel_bias, lq1, lk1, lq2, lk2, subln_g, w_o_moba, w_o_diff, w_out,
           ln1_g, ln1_b, peer_wq, keys1, keys2, peer_u, peer_v, ln2_g, ln2_b):
    bf16 = jnp.bfloat16
    f32 = jnp.float32
    lambda_init = 0.8 - 0.6 * math.exp(-0.3 * layer)
    xb = x.astype(bf16)

    c0 = 2 * MOBA_WIDTH
    c1 = c0 + MOBA_WIDTH + 3 * DIFF_WIDTH
    col_scale = jnp.concatenate([jnp.ones((MOBA_WIDTH,), f32), jnp.full((DIFF_WIDTH,), QK_SCALE_LOG2, f32),
                                 jnp.ones((2 * DIFF_WIDTH,), f32)]).reshape(1, c1 - c0)
    mqk = _matmul(xb, w_in, 0, c0, f32, 512, 1024, "proj_moba_qk")
    mid = _matmul(xb, w_in, c0, c1 - c0, bf16, 512, 1024, "proj_attn", col_scale)
    gates = _matmul(xb, w_in, c1, 2 * D_MODEL, f32, 512, 1024, "proj_gates")

    b0, b1 = _bias_tiles(rel_bias, BIAS_TILE)
    qa, ka = _moba_route(mqk)
    y_a = _moba_attention(qa, ka, mid, 0, b0, b1)

    lam = (jnp.exp(jnp.sum(lq1.astype(f32) * lk1.astype(f32)))
           - jnp.exp(jnp.sum(lq2.astype(f32) * lk2.astype(f32))) + lambda_init).reshape(1, 1)
    hb = MOBA_WIDTH // HEAD_DIM
    y_b = _diff_attention(lam, mid, hb, hb + DIFF_WIDTH // HEAD_DIM,
                          (MOBA_WIDTH + 2 * DIFF_WIDTH) // DIFF_V_DIM, b0, b1,
                          subln_g.reshape(1, DIFF_V_DIM).astype(f32), 1.0 - lambda_init)

    merged = _gated_merge(y_a, y_b, w_o_moba, w_o_diff, gates,
                          gate_b.reshape(1, 2 * D_MODEL).astype(f32))
    h = _matmul(merged, w_out, 0, D_MODEL, f32, 512, 1024, "proj_out")
    x1, x1b = _residual_ln(x, h, ln1_g.reshape(1, D_MODEL), ln1_b.reshape(1, D_MODEL), (f32, bf16))

    qp = _matmul(x1b, peer_wq, 0, PEER_HEADS * PEER_QUERY_DIM, f32, 512, 1024, "peer_query")
    c1, e1, r2, e2 = _peer_route(qp, keys1, keys2)
    h2 = _peer_experts(x1b, peer_u.astype(bf16), peer_v.astype(bf16), c1, e1, r2, e2)
    (x2,) = _residual_ln(x1, h2, ln2_g.reshape(1, D_MODEL), ln2_b.reshape(1, D_MODEL), (f32,))
    return x2


def kernel(x, w_in, gate_b, rel_bias, lambda_q1, lambda_k1, lambda_q2, lambda_k2, subln_g, w_o_moba,
           w_o_diff, w_out, ln1_g, ln1_b, peer_wq, peer_keys1, peer_keys2, peer_u, peer_v, ln2_g, ln2_b):
    b, s, d = x.shape
    assert (b, s, d) == (1, SEQ, D_MODEL) and w_in.shape[0] == DEPTH
    xs = x.reshape(s, d)
    for layer in range(DEPTH):
        xs = _layer(xs, layer, w_in[layer], gate_b[layer], rel_bias, lambda_q1[layer], lambda_k1[layer],
                    lambda_q2[layer], lambda_k2[layer], subln_g[layer], w_o_moba[layer], w_o_diff[layer],
                    w_out[layer], ln1_g[layer], ln1_b[layer], peer_wq[layer], peer_keys1[layer],
                    peer_keys2[layer], peer_u[layer], peer_v[layer], ln2_g[layer], ln2_b[layer])
    return xs.reshape(b, s, d)
```

```python
import functools
import math
from typing import Any, NamedTuple

import jax
import jax.numpy as jnp
import numpy as np
from jax import lax
from jax.experimental import pallas as pl
from jax.experimental.pallas import tpu as pltpu

D_MODEL = 4096
SEQ = 8192
DEPTH = 1
HEAD_DIM = 128
MOBA_HEADS = 16
MOBA_WIDTH = MOBA_HEADS * HEAD_DIM
MOBA_BLOCK = 256
MOBA_TOPK = 3
N_MOBA_BLOCKS = SEQ // MOBA_BLOCK
DIFF_HEADS = 8
DIFF_V_DIM = 2 * HEAD_DIM
DIFF_WIDTH = DIFF_HEADS * DIFF_V_DIM
N_BUCKETS = 32
MAX_DISTANCE = 128
PEER_HEADS = 8
PEER_N_KEYS = 128
PEER_HALF = 128
PEER_QUERY_DIM = 2 * PEER_HALF
PEER_TOPK = 16
DEEPNORM_ALPHA = (2.0 * DEPTH) ** 0.25
LN_EPS = 1e-5
NEG_INF = -1e30

LOG2E = math.log2(math.e)
QK_SCALE_LOG2 = HEAD_DIM ** -0.5 * LOG2E

LANES = 128
MXU_DIM = 256
VMEM_LIMIT_CAP = 60000 * 1024
COMPILER_SCRATCH_BYTES = 8 << 20

ATTN_Q_TILE = 1024
ATTN_K_TILE = 512
BIAS_TILE = LANES
ATTN_ROW_CHUNK = 32

_NT = (((1,), (1,)), ((), ()))
_TN = (((0,), (0,)), ((), ()))


def _nbytes(shape, dtype):
    return int(np.prod(shape)) * jnp.dtype(dtype).itemsize


def _params(semantics, pipelined=(), resident=()):
    need = 2 * sum(pipelined) + sum(resident) + COMPILER_SCRATCH_BYTES
    return pltpu.CompilerParams(dimension_semantics=semantics,
                                vmem_limit_bytes=min(need, VMEM_LIMIT_CAP))


_NN = (((1,), (0,)), ((), ()))


def _matmul_kernel(a_ref, b_ref, *rest):
    *scale, o_ref = rest
    acc = lax.dot_general(a_ref[...], b_ref[...], _NN, preferred_element_type=jnp.float32)
    for s_ref in scale:
        acc = acc * s_ref[...]
    o_ref[...] = acc.astype(o_ref.dtype)


def _matmul(a, w, col0, n, out_dtype, tm, tn, name, col_scale=None):
    m, k = a.shape
    c0 = col0 // tn
    scale_specs = [] if col_scale is None else [pl.BlockSpec((1, tn), lambda j, i: (0, j))]
    scale_args = [] if col_scale is None else [col_scale]
    return pl.pallas_call(
        _matmul_kernel,
        grid=(n // tn, m // tm),
        in_specs=[pl.BlockSpec((tm, k), lambda j, i: (i, 0)),
                  pl.BlockSpec((k, tn), lambda j, i: (0, c0 + j))] + scale_specs,
        out_specs=pl.BlockSpec((tm, tn), lambda j, i: (i, j)),
        out_shape=jax.ShapeDtypeStruct((m, n), out_dtype),
        compiler_params=_params(("arbitrary", "arbitrary"),
                                (_nbytes((tm, k), a.dtype), _nbytes((k, tn), w.dtype),
                                 _nbytes((tm, tn), out_dtype))),
        name=name,
    )(a, w, *scale_args)


def _moba_route_kernel(q_ref, k_ref, qa_ref, ka_ref, kmean_ref, *, tq):
    i = pl.program_id(1)

    @pl.when(i == 0)
    def _():
        k3 = k_ref[...].reshape(N_MOBA_BLOCKS, MOBA_BLOCK, HEAD_DIM)
        kmean_ref[...] = jnp.zeros_like(kmean_ref)
        kmean_ref[0:N_MOBA_BLOCKS, :] = jnp.sum(k3, axis=1) * (1.0 / MOBA_BLOCK)

    q = q_ref[...]
    gate = lax.dot_general(q, kmean_ref[...], _NT, precision=lax.Precision.HIGHEST,
                           preferred_element_type=jnp.float32)
    lane = lax.broadcasted_iota(jnp.int32, (tq, LANES), 1)
    row = lax.broadcasted_iota(jnp.int32, (tq, LANES), 0)
    own = jnp.right_shift(i * tq + row, int(math.log2(MOBA_BLOCK)))
    lane_f = lane.astype(jnp.float32)
    past = lane < own
    g = jnp.where(past, gate, NEG_INF)
    picked = jnp.zeros((tq, LANES), jnp.float32)
    for _ in range(MOBA_TOPK):
        mx = jnp.max(g, axis=1, keepdims=True)
        first = jnp.min(jnp.where(g == mx, lane_f, float(LANES)), axis=1, keepdims=True)
        hit = lane_f == first
        picked = jnp.where(hit, 1.0, picked)
        g = jnp.where(hit, -jnp.inf, g)
    allowed = jnp.where(past, picked, 0.0)
    allowed = jnp.where(lane == own, 1.0, allowed)
    blocked = jnp.where(lane < N_MOBA_BLOCKS, 1.0 - allowed, 0.0)
    qa_ref[:, :HEAD_DIM] = (q * QK_SCALE_LOG2).astype(qa_ref.dtype)
    qa_ref[:, HEAD_DIM:] = blocked.astype(qa_ref.dtype)
    kb = k_ref[pl.ds(pl.multiple_of(i * tq, tq), tq), :]
    ka_ref[:, :HEAD_DIM] = kb.astype(ka_ref.dtype)
    ka_ref[:, HEAD_DIM:] = jnp.where(lane == own, NEG_INF, 0.0).astype(ka_ref.dtype)


def _moba_route(mqk, tq=1024):
    s = mqk.shape[0]
    aug = 2 * HEAD_DIM
    out = jax.ShapeDtypeStruct((s, MOBA_HEADS * aug), jnp.bfloat16)
    return pl.pallas_call(
        functools.partial(_moba_route_kernel, tq=tq),
        grid=(MOBA_HEADS, s // tq),
        in_specs=[pl.BlockSpec((tq, HEAD_DIM), lambda h, i: (i, h)),
                  pl.BlockSpec((s, HEAD_DIM), lambda h, i: (0, MOBA_HEADS + h))],
        out_specs=[pl.BlockSpec((tq, aug), lambda h, i: (i, h)),
                   pl.BlockSpec((tq, aug), lambda h, i: (i, h))],
        out_shape=[out, out],
        scratch_shapes=[pltpu.VMEM((LANES, HEAD_DIM), jnp.float32)],
        compiler_params=_params(("arbitrary", "arbitrary"),
                                (_nbytes((tq, HEAD_DIM), jnp.float32),
                                 _nbytes((s, HEAD_DIM), jnp.float32),
                                 2 * _nbytes((tq, aug), jnp.bfloat16))),
        name="moba_route",
    )(mqk, mqk)


def _t5_bucket(dist, xp):
    n = xp.maximum(dist, 0)
    max_exact = N_BUCKETS // 2
    nf = xp.maximum(n, 1).astype(xp.float32)
    large = max_exact + (xp.log(nf / max_exact) / math.log(MAX_DISTANCE / max_exact)
                         * (N_BUCKETS - max_exact)).astype(xp.int32)
    large = xp.minimum(large, N_BUCKETS - 1)
    return xp.where(n < max_exact, n, large)


def _toeplitz(g, t):
    heads = g.shape[0]
    rows = jnp.tile(g, (1, t))[:, :t * (2 * t - 1)].reshape(heads, t, 2 * t - 1)
    return rows[:, :, :t]


def _bias_tiles(rel_bias, t):
    assert int(_t5_bucket(np.array([t + 1]), np)[0]) == N_BUCKETS - 1
    tab = rel_bias.T.astype(jnp.float32)
    rel = (tab - tab[:, N_BUCKETS - 1:]) * LOG2E
    c = jnp.arange(2 * t, dtype=jnp.int32)
    d0 = jnp.where(c == 0, 0, 2 * t - c)
    g0 = jnp.where((c == 0) | (c > t), rel[:, _t5_bucket(d0, jnp)], NEG_INF)
    d1 = jnp.where(c < t, t - c, 3 * t - c)
    g1 = rel[:, _t5_bucket(d1, jnp)]
    return _toeplitz(g0, t), _toeplitz(g1, t)


class _Stream(NamedTuple):
    q_ref: Any
    k_ref: Any
    v_ref: Any
    b0_ref: Any
    b1_ref: Any
    s_ref: Any
    p_ref: Any
    m_ref: Any
    a_ref: Any
    l_ref: Any
    acc_ref: Any


_STREAM_SCRATCH = ("s_ref", "p_ref", "m_ref", "a_ref", "l_ref", "acc_ref")


def _stream_scratch(tq, tk, dv):
    return [pltpu.VMEM((tq, tk), jnp.float32), pltpu.VMEM((tq, tk), jnp.bfloat16),
            pltpu.VMEM((tq, LANES), jnp.float32), pltpu.VMEM((tq, LANES), jnp.float32),
            pltpu.VMEM((tq, LANES), jnp.float32), pltpu.VMEM((tq, dv), jnp.float32)]


def _stream_scratch_bytes(tq, tk, dv):
    return (_nbytes((tq, tk), jnp.float32) + _nbytes((tq, tk), jnp.bfloat16)
            + 3 * _nbytes((tq, LANES), jnp.float32) + _nbytes((tq, dv), jnp.float32))


def _attn_step(streams, j, lead, tq, tk):
    off = pl.multiple_of(j * tk, tk)
    n_blocks = tk // BIAS_TILE
    row0 = 0 if lead is None else max(0, -lead) * BIAS_TILE
    for st in streams:
        st.s_ref[row0:, :] = lax.dot_general(st.q_ref[row0:, :], st.k_ref[pl.ds(off, tk), :], _NT,
                                             preferred_element_type=jnp.float32)
    for st in streams:
        for r in range(row0 // ATTN_ROW_CHUNK, tq // ATTN_ROW_CHUNK):
            rows = slice(r * ATTN_ROW_CHUNK, (r + 1) * ATTN_ROW_CHUNK)
            qb, in_block = divmod(r * ATTN_ROW_CHUNK, BIAS_TILE)
            bias_rows = slice(in_block, in_block + ATTN_ROW_CHUNK)
            blocks = {}
            for c in range(n_blocks):
                gap = 2 if lead is None else qb + lead - c
                if gap < 0:
                    continue
                blk = st.s_ref[rows, c * BIAS_TILE:(c + 1) * BIAS_TILE]
                if gap == 0:
                    blk = blk + st.b0_ref[bias_rows, :]
                elif gap == 1:
                    blk = blk + st.b1_ref[bias_rows, :]
                blocks[c] = blk
            mx = functools.reduce(jnp.maximum, blocks.values())
            m_prev = st.m_ref[rows, :]
            m_new = jnp.maximum(m_prev, jnp.max(mx, axis=1, keepdims=True))
            alpha = jnp.exp2(m_prev - m_new)
            probs = {c: jnp.exp2(blk - m_new) for c, blk in blocks.items()}
            st.m_ref[rows, :] = m_new
            st.a_ref[rows, :] = alpha
            st.l_ref[rows, :] = alpha * st.l_ref[rows, :] + functools.reduce(jnp.add, probs.values())
            for c in range(n_blocks):
                p = probs[c].astype(st.p_ref.dtype) if c in probs else jnp.zeros(
                    (ATTN_ROW_CHUNK, BIAS_TILE), st.p_ref.dtype)
                st.p_ref[rows, c * BIAS_TILE:(c + 1) * BIAS_TILE] = p
    for st in streams:
        pv = jnp.dot(st.p_ref[row0:, :], st.v_ref[pl.ds(off, tk), :], preferred_element_type=jnp.float32)
        for c in range(pv.shape[1] // LANES):
            cols = slice(c * LANES, (c + 1) * LANES)
            st.acc_ref[row0:, cols] = st.a_ref[row0:, :] * st.acc_ref[row0:, cols] + pv[:, cols]


def _attend(streams, i, tq, tk):
    for st in streams:
        st.m_ref[...] = jnp.full_like(st.m_ref, -jnp.inf)
        st.l_ref[...] = jnp.zeros_like(st.l_ref)
        st.acc_ref[...] = jnp.zeros_like(st.acc_ref)

    ratio = tq // tk
    first = ratio * i
    blocks_per_tile = tk // BIAS_TILE

    def far(j, carry):
        _attn_step(streams, j, None, tq, tk)
        return carry

    lax.fori_loop(0, jnp.maximum(first - 1, 0), far, 0)

    @pl.when(i > 0)
    def _():
        _attn_step(streams, first - 1, blocks_per_tile, tq, tk)

    for d in range(ratio):
        @pl.when(first + d >= 0)
        def _(d=d):
            _attn_step(streams, first + d, -d * blocks_per_tile, tq, tk)
    return [st.acc_ref[...] / jnp.sum(st.l_ref[...], axis=1, keepdims=True) for st in streams]


def _moba_attn_kernel(qa_ref, qb_ref, ka_ref, kb_ref, va_ref, vb_ref, b0a_ref, b0b_ref, b1a_ref, b1b_ref,
                      o_ref, *scratch, tq, tk):
    n = len(_STREAM_SCRATCH)
    streams = [_Stream(qa_ref, ka_ref, va_ref, b0a_ref, b1a_ref, *scratch[:n]),
               _Stream(qb_ref, kb_ref, vb_ref, b0b_ref, b1b_ref, *scratch[n:])]
    oa, ob = _attend(streams, pl.program_id(1), tq, tk)
    o_ref[:, :HEAD_DIM] = oa.astype(o_ref.dtype)
    o_ref[:, HEAD_DIM:] = ob.astype(o_ref.dtype)


def _moba_attention(qa, ka, v, v_col0, b0, b1, tq=ATTN_Q_TILE, tk=ATTN_K_TILE):
    s = qa.shape[0]
    aug = 2 * HEAD_DIM

    def pair(spec_of_head):
        return [spec_of_head(0), spec_of_head(1)]

    return pl.pallas_call(
        functools.partial(_moba_attn_kernel, tq=tq, tk=tk),
        grid=(MOBA_HEADS // 2, s // tq),
        in_specs=(pair(lambda d: pl.BlockSpec((tq, aug), lambda g, i: (i, 2 * g + d)))
                  + pair(lambda d: pl.BlockSpec((s, aug), lambda g, i: (0, 2 * g + d)))
                  + pair(lambda d: pl.BlockSpec((s, HEAD_DIM), lambda g, i: (0, v_col0 + 2 * g + d)))
                  + pair(lambda d: pl.BlockSpec((None, BIAS_TILE, BIAS_TILE), lambda g, i: (2 * g + d, 0, 0)))
                  + pair(lambda d: pl.BlockSpec((None, BIAS_TILE, BIAS_TILE), lambda g, i: (2 * g + d, 0, 0)))),
        out_specs=pl.BlockSpec((tq, 2 * HEAD_DIM), lambda g, i: (i, g)),
        out_shape=jax.ShapeDtypeStruct((s, MOBA_WIDTH), jnp.bfloat16),
        scratch_shapes=2 * _stream_scratch(tq, tk, HEAD_DIM),
        compiler_params=_params(("arbitrary", "arbitrary"),
                                (2 * _nbytes((tq, aug), jnp.bfloat16), 2 * _nbytes((s, aug), jnp.bfloat16),
                                 2 * _nbytes((s, HEAD_DIM), jnp.bfloat16),
                                 4 * _nbytes((BIAS_TILE, BIAS_TILE), jnp.float32),
                                 _nbytes((tq, 2 * HEAD_DIM), jnp.bfloat16)),
                                (2 * _stream_scratch_bytes(tq, tk, HEAD_DIM),)),
        name="moba_attention",
    )(qa, qa, ka, ka, v, v, b0, b0, b1, b1)


def _diff_attn_kernel(lam_ref, q1_ref, q2_ref, k1_ref, k2_ref, v_ref, b0_ref, b1_ref, g_ref, o_ref,
                      *scratch, tq, tk, out_scale):
    n = len(_STREAM_SCRATCH)
    streams = [_Stream(q1_ref, k1_ref, v_ref, b0_ref, b1_ref, *scratch[:n]),
               _Stream(q2_ref, k2_ref, v_ref, b0_ref, b1_ref, *scratch[n:])]
    o1, o2 = _attend(streams, pl.program_id(1), tq, tk)
    y = o1 - lam_ref[0, 0] * o2
    y = y * lax.rsqrt(jnp.mean(y * y, axis=1, keepdims=True) + LN_EPS) * g_ref[...] * out_scale
    o_ref[...] = y.astype(o_ref.dtype)


def _diff_attention(lam, proj, q_col0, k_col0, v_col0, b0, b1, subln_g, out_scale,
                    tq=ATTN_Q_TILE, tk=ATTN_K_TILE):
    s = proj.shape[0]
    return pl.pallas_call(
        functools.partial(_diff_attn_kernel, tq=tq, tk=tk, out_scale=out_scale),
        grid=(DIFF_HEADS, s // tq),
        in_specs=[pl.BlockSpec(memory_space=pltpu.SMEM),
                  pl.BlockSpec((tq, HEAD_DIM), lambda h, i: (i, q_col0 + 2 * h)),
                  pl.BlockSpec((tq, HEAD_DIM), lambda h, i: (i, q_col0 + 2 * h + 1)),
                  pl.BlockSpec((s, HEAD_DIM), lambda h, i: (0, k_col0 + 2 * h)),
                  pl.BlockSpec((s, HEAD_DIM), lambda h, i: (0, k_col0 + 2 * h + 1)),
                  pl.BlockSpec((s, DIFF_V_DIM), lambda h, i: (0, v_col0 + h)),
                  pl.BlockSpec((None, BIAS_TILE, BIAS_TILE), lambda h, i: (MOBA_HEADS + h, 0, 0)),
                  pl.BlockSpec((None, BIAS_TILE, BIAS_TILE), lambda h, i: (MOBA_HEADS + h, 0, 0)),
                  pl.BlockSpec((1, DIFF_V_DIM), lambda h, i: (0, 0))],
        out_specs=pl.BlockSpec((tq, DIFF_V_DIM), lambda h, i: (i, h)),
        out_shape=jax.ShapeDtypeStruct((s, DIFF_WIDTH), jnp.bfloat16),
        scratch_shapes=2 * _stream_scratch(tq, tk, DIFF_V_DIM),
        compiler_params=_params(("arbitrary", "arbitrary"),
                                (2 * _nbytes((tq, HEAD_DIM), jnp.bfloat16),
                                 2 * _nbytes((s, HEAD_DIM), jnp.bfloat16),
                                 _nbytes((s, DIFF_V_DIM), jnp.bfloat16),
                                 2 * _nbytes((BIAS_TILE, BIAS_TILE), jnp.float32),
                                 _nbytes((tq, DIFF_V_DIM), jnp.bfloat16)),
                                (2 * _stream_scratch_bytes(tq, tk, DIFF_V_DIM),)),
        name="diff_attention",
    )(lam, proj, proj, proj, proj, proj, b0, b1, subln_g)


def _sigmoid(x):
    return 1.0 / (1.0 + jnp.exp(-x))


def _merge_kernel(ya_ref, yb_ref, wa_ref, wb_ref, ga_ref, gb_ref, ba_ref, bb_ref, o_ref):
    pa = lax.dot_general(ya_ref[...], wa_ref[...], _NN, preferred_element_type=jnp.float32)
    pb = lax.dot_general(yb_ref[...], wb_ref[...], _NN, preferred_element_type=jnp.float32)
    merged = (_sigmoid(ga_ref[...] + ba_ref[...]) * pa + _sigmoid(gb_ref[...] + bb_ref[...]) * pb)
    o_ref[...] = merged.astype(o_ref.dtype)


def _gated_merge(ya, yb, wa, wb, gates, gate_b, tm=256, tn=1024):
    s, k = ya.shape
    n = wa.shape[1]
    nb = n // tn
    return pl.pallas_call(
        _merge_kernel,
        grid=(nb, s // tm),
        in_specs=[pl.BlockSpec((tm, k), lambda j, i: (i, 0)),
                  pl.BlockSpec((tm, k), lambda j, i: (i, 0)),
                  pl.BlockSpec((k, tn), lambda j, i: (0, j)),
                  pl.BlockSpec((k, tn), lambda j, i: (0, j)),
                  pl.BlockSpec((tm, tn), lambda j, i: (i, j)),
                  pl.BlockSpec((tm, tn), lambda j, i: (i, nb + j)),
                  pl.BlockSpec((1, tn), lambda j, i: (0, j)),
                  pl.BlockSpec((1, tn), lambda j, i: (0, nb + j))],
        out_specs=pl.BlockSpec((tm, tn), lambda j, i: (i, j)),
        out_shape=jax.ShapeDtypeStruct((s, n), jnp.bfloat16),
        compiler_params=_params(("arbitrary", "arbitrary"),
                                (2 * _nbytes((tm, k), ya.dtype), 2 * _nbytes((k, tn), wa.dtype),
                                 2 * _nbytes((tm, tn), gates.dtype),
                                 _nbytes((tm, tn), jnp.bfloat16))),
        name="gated_merge",
    )(ya, yb, wa, wb, gates, gates, gate_b, gate_b)


def _residual_ln_kernel(x_ref, h_ref, g_ref, b_ref, *o_refs):
    z = DEEPNORM_ALPHA * x_ref[...] + h_ref[...]
    mu = jnp.mean(z, axis=1, keepdims=True)
    zc = z - mu
    var = jnp.mean(zc * zc, axis=1, keepdims=True)
    y = zc * lax.rsqrt(var + LN_EPS) * g_ref[...] + b_ref[...]
    for o_ref in o_refs:
        o_ref[...] = y.astype(o_ref.dtype)


def _residual_ln(x, h, g, b, out_dtypes, tm=256):
    s, d = x.shape
    row = pl.BlockSpec((tm, d), lambda i: (i, 0))
    vec = pl.BlockSpec((1, d), lambda i: (0, 0))
    return pl.pallas_call(
        _residual_ln_kernel,
        grid=(s // tm,),
        in_specs=[row, row, vec, vec],
        out_specs=[row for _ in out_dtypes],
        out_shape=[jax.ShapeDtypeStruct((s, d), dt) for dt in out_dtypes],
        compiler_params=_params(("arbitrary",),
                                [2 * _nbytes((tm, d), jnp.float32)]
                                + [_nbytes((tm, d), dt) for dt in out_dtypes]),
        name="residual_layernorm",
    )(x, h, g, b)


def _top_rows_one_by_one(s, k):
    n = s.shape[0]
    ridx = lax.broadcasted_iota(jnp.int32, s.shape, 0).astype(jnp.float32)
    rows, rank = [], jnp.full(s.shape, float(k), jnp.float32)
    for i in range(k):
        mx = jnp.max(s, axis=0, keepdims=True)
        rows.append(mx)
        first = jnp.min(jnp.where(s == mx, ridx, float(n)), axis=0, keepdims=True)
        hit = ridx == first
        rank = jnp.where(hit, float(i), rank)
        s = jnp.where(hit, -jnp.inf, s)
    return tuple(rows) + (rank,)


def _top_rows(s, k):
    rows, rest, rank = [], s, jnp.full(s.shape, float(k), jnp.float32)
    for i in range(k):
        mx = jnp.max(rest, axis=0, keepdims=True)
        rows.append(mx)
        hit = rest == mx
        rank = jnp.where(hit, float(i), rank)
        rest = jnp.where(hit, -jnp.inf, rest)
    removed = jnp.sum(jnp.where(rest == -jnp.inf, 1.0, 0.0), axis=0, keepdims=True)
    out = lax.cond(jnp.max(removed) > k, lambda: _top_rows_one_by_one(s, k),
                   lambda: tuple(rows) + (rank,))
    return out[:k], out[k]


def _stack_rows(rows):
    k, tm = len(rows), rows[0].shape[1]
    ridx = lax.broadcasted_iota(jnp.int32, (k, tm), 0)
    out = jnp.zeros((k, tm), rows[0].dtype)
    for r, row in enumerate(rows):
        out = jnp.where(ridx == r, row, out)
    return out


def _peer_route_kernel(q_ref, k1_ref, k2_ref, c1_ref, e1_ref, r2_ref, e2_ref, *, tm):
    q = q_ref[...]
    s1 = lax.dot_general(k1_ref[...], q[:, :PEER_HALF], _NT, precision=lax.Precision.HIGHEST,
                         preferred_element_type=jnp.float32)
    s2 = lax.dot_general(k2_ref[...], q[:, PEER_HALF:], _NT, precision=lax.Precision.HIGHEST,
                         preferred_element_type=jnp.float32)
    v1, rank1 = _top_rows(s1, PEER_TOPK)
    v2, rank2 = _top_rows(s2, PEER_TOPK)
    v1m = _stack_rows(v1)
    v2m = _stack_rows(v2)
    half = PEER_TOPK // 2
    assert all((a + 1) * (b + 1) > PEER_TOPK for a in range(1, half) for b in range(half, PEER_TOPK))
    assert all((a + 1) * (b + 1) > PEER_TOPK for a in range(half, PEER_TOPK) for b in range(1, PEER_TOPK))
    cand = jnp.concatenate([v1[0] + v2m] + [v1[a] + v2m[:half] for a in range(1, half)]
                           + [v1m[half:] + v2[0]], axis=0)
    top, _ = _top_rows(cand, PEER_TOPK)
    thr = top[PEER_TOPK - 1]
    sums = [v1[a] + v2m for a in range(PEER_TOPK)]
    above = [jnp.sum(jnp.where(sm > thr, 1.0, 0.0), axis=0, keepdims=True) for sm in sums]
    equal = [jnp.sum(jnp.where(sm == thr, 1.0, 0.0), axis=0, keepdims=True) for sm in sums]
    spare = float(PEER_TOPK) - functools.reduce(jnp.add, above)
    bidx = lax.broadcasted_iota(jnp.int32, (PEER_TOPK, tm), 0).astype(jnp.float32)
    c1 = jnp.zeros_like(s1)
    z = jnp.zeros_like(thr)
    for a in range(PEER_TOPK):
        taken = jnp.minimum(equal[a], spare)
        spare = spare - taken
        count = above[a] + taken
        z = z + jnp.sum(jnp.where(bidx < count, jnp.exp(sums[a] - top[0]), 0.0), axis=0, keepdims=True)
        c1 = jnp.where(rank1 == float(a), count, c1)
    c1_ref[...] = c1
    e1_ref[...] = jnp.exp(s1 - v1[0])
    r2_ref[...] = rank2.astype(r2_ref.dtype)
    e2_ref[...] = (jnp.exp(s2 - v2[0]) / z).astype(e2_ref.dtype)


def _peer_route(qp, keys1, keys2, tm=512):
    s = qp.shape[0]
    shape = (PEER_HEADS, PEER_N_KEYS, s)
    spec = pl.BlockSpec((None, PEER_N_KEYS, tm), lambda i, h: (h, 0, i))
    key_spec = pl.BlockSpec((PEER_N_KEYS, PEER_HALF), lambda i, h: (0, 0))
    return pl.pallas_call(
        functools.partial(_peer_route_kernel, tm=tm),
        grid=(s // tm, PEER_HEADS),
        in_specs=[pl.BlockSpec((tm, PEER_QUERY_DIM), lambda i, h: (i, h)), key_spec, key_spec],
        out_specs=[spec, spec, spec, spec],
        out_shape=[jax.ShapeDtypeStruct(shape, jnp.float32), jax.ShapeDtypeStruct(shape, jnp.float32),
                   jax.ShapeDtypeStruct(shape, jnp.bfloat16), jax.ShapeDtypeStruct(shape, jnp.bfloat16)],
        compiler_params=_params(("arbitrary", "arbitrary"),
                                (_nbytes((tm, PEER_QUERY_DIM), jnp.float32),
                                 2 * _nbytes((PEER_N_KEYS, PEER_HALF), jnp.float32),
                                 3 * _nbytes((PEER_N_KEYS, tm), jnp.float32))),
        name="peer_route",
    )(qp, keys1, keys2)


def _gelu(x):
    return 0.5 * x * (1.0 + lax.erf(x * math.sqrt(0.5)))


def _peer_expert_kernel(x_ref, u_ref, v_ref, c1_ref, e1_ref, r2_ref, e2_ref, o_ref, *w_refs, tm, te):
    e = pl.program_id(1)

    @pl.when(e == 0)
    def _():
        o_ref[...] = jnp.zeros_like(o_ref)

    gate_dtype = r2_ref.dtype
    groups = MXU_DIM // PEER_N_KEYS
    outs = []
    for chain, w_ref in enumerate(w_refs):
        base = chain * MXU_DIM
        act = lax.dot_general(u_ref[base:base + MXU_DIM, :], x_ref[...], _NT,
                              preferred_element_type=jnp.float32)
        act = _gelu(act).astype(gate_dtype)
        for c in range(groups):
            i1 = e * (te // PEER_N_KEYS) + chain * groups + c
            g = None
            for h in range(PEER_HEADS):
                c1_row = c1_ref[h, pl.ds(i1, 1), :].astype(gate_dtype)
                e1_row = e1_ref[h, pl.ds(i1, 1), :].astype(gate_dtype)
                term = jnp.where(r2_ref[h] < c1_row, e2_ref[h], 0.0) * e1_row
                g = term if g is None else g + term
            lo = c * PEER_N_KEYS
            w_ref[lo:lo + PEER_N_KEYS, :] = g * act[lo:lo + PEER_N_KEYS, :]
        outs.append(lax.dot_general(w_ref[...], v_ref[base:base + MXU_DIM, :], _TN,
                                    preferred_element_type=jnp.float32))
    o_ref[...] += functools.reduce(jnp.add, outs)


def _peer_experts(x, u, v, c1, e1, r2, e2, tm=512, te=512):
    s, d = x.shape
    n_exp = u.shape[0]
    once = pl.Buffered(1)
    gate_spec = pl.BlockSpec((PEER_HEADS, PEER_N_KEYS, tm), lambda i, e: (0, 0, i), pipeline_mode=once)
    gate_bytes = sum(_nbytes((PEER_HEADS, PEER_N_KEYS, tm), a.dtype) for a in (c1, e1, r2, e2))
    return pl.pallas_call(
        functools.partial(_peer_expert_kernel, tm=tm, te=te),
        grid=(s // tm, n_exp // te),
        in_specs=[pl.BlockSpec((tm, d), lambda i, e: (i, 0), pipeline_mode=once),
                  pl.BlockSpec((te, d), lambda i, e: (e, 0)),
                  pl.BlockSpec((te, d), lambda i, e: (e, 0)),
                  gate_spec, gate_spec, gate_spec, gate_spec],
        out_specs=pl.BlockSpec((tm, d), lambda i, e: (i, 0)),
        out_shape=jax.ShapeDtypeStruct((s, d), jnp.float32),
        scratch_shapes=[pltpu.VMEM((MXU_DIM, tm), r2.dtype) for _ in range(te // MXU_DIM)],
        compiler_params=_params(("arbitrary", "arbitrary"),
                                (2 * _nbytes((te, d), u.dtype), _nbytes((tm, d), jnp.float32)),
                                (_nbytes((tm, d), x.dtype), gate_bytes, _nbytes((te, tm), r2.dtype))),
        name="peer_experts",
    )(x, u, v, c1, e1, r2, e2)


def _layer(x, layer, w_in, gate_b, rel_bias, lq1, lk1, lq2, lk2, subln_g, w_o_moba, w_o_diff, w_out,
           ln1_g, ln1_b, peer_wq, keys1, keys2, peer_u, peer_v, ln2_g, ln2_b):
    bf16 = jnp.bfloat16
    f32 = jnp.float32
    lambda_init = 0.8 - 0.6 * math.exp(-0.3 * layer)
    xb = x.astype(bf16)

    c0 = 2 * MOBA_WIDTH
    c1 = c0 + MOBA_WIDTH + 3 * DIFF_WIDTH
    col_scale = jnp.concatenate([jnp.ones((MOBA_WIDTH,), f32), jnp.full((DIFF_WIDTH,), QK_SCALE_LOG2, f32),
                                 jnp.ones((2 * DIFF_WIDTH,), f32)]).reshape(1, c1 - c0)
    mqk = _matmul(xb, w_in, 0, c0, f32, 512, 1024, "proj_moba_qk")
    mid = _matmul(xb, w_in, c0, c1 - c0, bf16, 512, 1024, "proj_attn", col_scale)
    gates = _matmul(xb, w_in, c1, 2 * D_MODEL, f32, 512, 1024, "proj_gates")

    b0, b1 = _bias_tiles(rel_bias, BIAS_TILE)
    qa, ka = _moba_route(mqk)
    y_a = _moba_attention(qa, ka, mid, 0, b0, b1)

    lam = (jnp.exp(jnp.sum(lq1.astype(f32) * lk1.astype(f32)))
           - jnp.exp(jnp.sum(lq2.astype(f32) * lk2.astype(f32))) + lambda_init).reshape(1, 1)
    hb = MOBA_WIDTH // HEAD_DIM
    y_b = _diff_attention(lam, mid, hb, hb + DIFF_WIDTH // HEAD_DIM,
                          (MOBA_WIDTH + 2 * DIFF_WIDTH) // DIFF_V_DIM, b0, b1,
                          subln_g.reshape(1, DIFF_V_DIM).astype(f32), 1.0 - lambda_init)

    merged = _gated_merge(y_a, y_b, w_o_moba, w_o_diff, gates,
                          gate_b.reshape(1, 2 * D_MODEL).astype(f32))
    h = _matmul(merged, w_out, 0, D_MODEL, f32, 512, 1024, "proj_out")
    x1, x1b = _residual_ln(x, h, ln1_g.reshape(1, D_MODEL), ln1_b.reshape(1, D_MODEL), (f32, bf16))

    qp = _matmul(x1b, peer_wq, 0, PEER_HEADS * PEER_QUERY_DIM, f32, 512, 1024, "peer_query")
    c1, e1, r2, e2 = _peer_route(qp, keys1, keys2)
    h2 = _peer_experts(x1b, peer_u.astype(bf16), peer_v.astype(bf16), c1, e1, r2, e2)
    (x2,) = _residual_ln(x1, h2, ln2_g.reshape(1, D_MODEL), ln2_b.reshape(1, D_MODEL), (f32,))
    return x2


def kernel(x, w_in, gate_b, rel_bias, lambda_q1, lambda_k1, lambda_q2, lambda_k2, subln_g, w_o_moba,
           w_o_diff, w_out, ln1_g, ln1_b, peer_wq, peer_keys1, peer_keys2, peer_u, peer_v, ln2_g, ln2_b):
    b, s, d = x.shape
    assert (b, s, d) == (1, SEQ, D_MODEL) and w_in.shape[0] == DEPTH
    xs = x.reshape(s, d)
    for layer in range(DEPTH):
        xs = _layer(xs, layer, w_in[layer], gate_b[layer], rel_bias, lambda_q1[layer], lambda_k1[layer],
                    lambda_q2[layer], lambda_k2[layer], subln_g[layer], w_o_moba[layer], w_o_diff[layer],
                    w_out[layer], ln1_g[layer], ln1_b[layer], peer_wq[layer], peer_keys1[layer],
                    peer_keys2[layer], peer_u[layer], peer_v[layer], ln2_g[layer], ln2_b[layer])
    return xs.reshape(b, s, d)
```

```python
import functools
import math
from typing import Any, NamedTuple

import jax
import jax.numpy as jnp
import numpy as np
from jax import lax
from jax.experimental import pallas as pl
from jax.experimental.pallas import tpu as pltpu

D_MODEL = 4096
SEQ = 8192
DEPTH = 1
HEAD_DIM = 128
MOBA_HEADS = 16
MOBA_WIDTH = MOBA_HEADS * HEAD_DIM
MOBA_BLOCK = 256
MOBA_TOPK = 3
N_MOBA_BLOCKS = SEQ // MOBA_BLOCK
DIFF_HEADS = 8
DIFF_V_DIM = 2 * HEAD_DIM
DIFF_WIDTH = DIFF_HEADS * DIFF_V_DIM
N_BUCKETS = 32
MAX_DISTANCE = 128
PEER_HEADS = 8
PEER_N_KEYS = 128
PEER_HALF = 128
PEER_QUERY_DIM = 2 * PEER_HALF
PEER_TOPK = 16
DEEPNORM_ALPHA = (2.0 * DEPTH) ** 0.25
LN_EPS = 1e-5
NEG_INF = -1e30

LOG2E = math.log2(math.e)
QK_SCALE_LOG2 = HEAD_DIM ** -0.5 * LOG2E

LANES = 128
MXU_DIM = 256
VMEM_LIMIT_CAP = 60000 * 1024
COMPILER_SCRATCH_BYTES = 8 << 20

ATTN_Q_TILE = 1024
ATTN_K_TILE = 512
BIAS_TILE = LANES
ATTN_ROW_CHUNK = 32

_NT = (((1,), (1,)), ((), ()))
_TN = (((0,), (0,)), ((), ()))


def _nbytes(shape, dtype):
    return int(np.prod(shape)) * jnp.dtype(dtype).itemsize


def _params(semantics, pipelined=(), resident=()):
    need = 2 * sum(pipelined) + sum(resident) + COMPILER_SCRATCH_BYTES
    return pltpu.CompilerParams(dimension_semantics=semantics,
                                vmem_limit_bytes=min(need, VMEM_LIMIT_CAP))


_NN = (((1,), (0,)), ((), ()))


def _matmul_kernel(a_ref, b_ref, *rest):
    *scale, o_ref = rest
    acc = lax.dot_general(a_ref[...], b_ref[...], _NN, preferred_element_type=jnp.float32)
    for s_ref in scale:
        acc = acc * s_ref[...]
    o_ref[...] = acc.astype(o_ref.dtype)


def _matmul(a, w, col0, n, out_dtype, tm, tn, name, col_scale=None):
    m, k = a.shape
    c0 = col0 // tn
    scale_specs = [] if col_scale is None else [pl.BlockSpec((1, tn), lambda j, i: (0, j))]
    scale_args = [] if col_scale is None else [col_scale]
    return pl.pallas_call(
        _matmul_kernel,
        grid=(n // tn, m // tm),
        in_specs=[pl.BlockSpec((tm, k), lambda j, i: (i, 0)),
                  pl.BlockSpec((k, tn), lambda j, i: (0, c0 + j))] + scale_specs,
        out_specs=pl.BlockSpec((tm, tn), lambda j, i: (i, j)),
        out_shape=jax.ShapeDtypeStruct((m, n), out_dtype),
        compiler_params=_params(("arbitrary", "arbitrary"),
                                (_nbytes((tm, k), a.dtype), _nbytes((k, tn), w.dtype),
                                 _nbytes((tm, tn), out_dtype))),
        name=name,
    )(a, w, *scale_args)


def _moba_route_kernel(q_ref, k_ref, qa_ref, ka_ref, kmean_ref, *, tq):
    i = pl.program_id(1)

    @pl.when(i == 0)
    def _():
        k3 = k_ref[...].reshape(N_MOBA_BLOCKS, MOBA_BLOCK, HEAD_DIM)
        kmean_ref[...] = jnp.zeros_like(kmean_ref)
        kmean_ref[0:N_MOBA_BLOCKS, :] = jnp.sum(k3, axis=1) * (1.0 / MOBA_BLOCK)

    q = q_ref[...]
    gate = lax.dot_general(q, kmean_ref[...], _NT, precision=lax.Precision.HIGHEST,
                           preferred_element_type=jnp.float32)
    lane = lax.broadcasted_iota(jnp.int32, (tq, LANES), 1)
    row = lax.broadcasted_iota(jnp.int32, (tq, LANES), 0)
    own = jnp.right_shift(i * tq + row, int(math.log2(MOBA_BLOCK)))
    lane_f = lane.astype(jnp.float32)
    past = lane < own
    g = jnp.where(past, gate, NEG_INF)
    picked = jnp.zeros((tq, LANES), jnp.float32)
    for _ in range(MOBA_TOPK):
        mx = jnp.max(g, axis=1, keepdims=True)
        first = jnp.min(jnp.where(g == mx, lane_f, float(LANES)), axis=1, keepdims=True)
        hit = lane_f == first
        picked = jnp.where(hit, 1.0, picked)
        g = jnp.where(hit, -jnp.inf, g)
    allowed = jnp.where(past, picked, 0.0)
    allowed = jnp.where(lane == own, 1.0, allowed)
    blocked = jnp.where(lane < N_MOBA_BLOCKS, 1.0 - allowed, 0.0)
    qa_ref[:, :HEAD_DIM] = (q * QK_SCALE_LOG2).astype(qa_ref.dtype)
    qa_ref[:, HEAD_DIM:] = blocked.astype(qa_ref.dtype)
    kb = k_ref[pl.ds(pl.multiple_of(i * tq, tq), tq), :]
    ka_ref[:, :HEAD_DIM] = kb.astype(ka_ref.dtype)
    ka_ref[:, HEAD_DIM:] = jnp.where(lane == own, NEG_INF, 0.0).astype(ka_ref.dtype)


def _moba_route(mqk, tq=1024):
    s = mqk.shape[0]
    aug = 2 * HEAD_DIM
    out = jax.ShapeDtypeStruct((s, MOBA_HEADS * aug), jnp.bfloat16)
    return pl.pallas_call(
        functools.partial(_moba_route_kernel, tq=tq),
        grid=(MOBA_HEADS, s // tq),
        in_specs=[pl.BlockSpec((tq, HEAD_DIM), lambda h, i: (i, h)),
                  pl.BlockSpec((s, HEAD_DIM), lambda h, i: (0, MOBA_HEADS + h))],
        out_specs=[pl.BlockSpec((tq, aug), lambda h, i: (i, h)),
                   pl.BlockSpec((tq, aug), lambda h, i: (i, h))],
        out_shape=[out, out],
        scratch_shapes=[pltpu.VMEM((LANES, HEAD_DIM), jnp.float32)],
        compiler_params=_params(("arbitrary", "arbitrary"),
                                (_nbytes((tq, HEAD_DIM), jnp.float32),
                                 _nbytes((s, HEAD_DIM), jnp.float32),
                                 2 * _nbytes((tq, aug), jnp.bfloat16))),
        name="moba_route",
    )(mqk, mqk)


def _t5_bucket(dist, xp):
    n = xp.maximum(dist, 0)
    max_exact = N_BUCKETS // 2
    nf = xp.maximum(n, 1).astype(xp.float32)
    large = max_exact + (xp.log(nf / max_exact) / math.log(MAX_DISTANCE / max_exact)
                         * (N_BUCKETS - max_exact)).astype(xp.int32)
    large = xp.minimum(large, N_BUCKETS - 1)
    return xp.where(n < max_exact, n, large)


def _toeplitz(g, t):
    heads = g.shape[0]
    rows = jnp.tile(g, (1, t))[:, :t * (2 * t - 1)].reshape(heads, t, 2 * t - 1)
    return rows[:, :, :t]


def _bias_tiles(rel_bias, t):
    assert int(_t5_bucket(np.array([t + 1]), np)[0]) == N_BUCKETS - 1
    tab = rel_bias.T.astype(jnp.float32)
    rel = (tab - tab[:, N_BUCKETS - 1:]) * LOG2E
    c = jnp.arange(2 * t, dtype=jnp.int32)
    d0 = jnp.where(c == 0, 0, 2 * t - c)
    g0 = jnp.where((c == 0) | (c > t), rel[:, _t5_bucket(d0, jnp)], NEG_INF)
    d1 = jnp.where(c < t, t - c, 3 * t - c)
    g1 = rel[:, _t5_bucket(d1, jnp)]
    return _toeplitz(g0, t), _toeplitz(g1, t)


class _Stream(NamedTuple):
    q_ref: Any
    k_ref: Any
    v_ref: Any
    b0_ref: Any
    b1_ref: Any
    s_ref: Any
    p_ref: Any
    m_ref: Any
    a_ref: Any
    l_ref: Any
    acc_ref: Any


_STREAM_SCRATCH = ("s_ref", "p_ref", "m_ref", "a_ref", "l_ref", "acc_ref")


def _stream_scratch(tq, tk, dv):
    return [pltpu.VMEM((tq, tk), jnp.float32), pltpu.VMEM((tq, tk), jnp.bfloat16),
            pltpu.VMEM((tq, LANES), jnp.float32), pltpu.VMEM((tq, LANES), jnp.float32),
            pltpu.VMEM((tq, LANES), jnp.float32), pltpu.VMEM((tq, dv), jnp.float32)]


def _stream_scratch_bytes(tq, tk, dv):
    return (_nbytes((tq, tk), jnp.float32) + _nbytes((tq, tk), jnp.bfloat16)
            + 3 * _nbytes((tq, LANES), jnp.float32) + _nbytes((tq, dv), jnp.float32))


def _attn_step(streams, j, lead, tq, tk):
    off = pl.multiple_of(j * tk, tk)
    n_blocks = tk // BIAS_TILE
    row0 = 0 if lead is None else max(0, -lead) * BIAS_TILE
    for st in streams:
        st.s_ref[row0:, :] = lax.dot_general(st.q_ref[row0:, :], st.k_ref[pl.ds(off, tk), :], _NT,
                                             preferred_element_type=jnp.float32)
    for st in streams:
        for r in range(row0 // ATTN_ROW_CHUNK, tq // ATTN_ROW_CHUNK):
            rows = slice(r * ATTN_ROW_CHUNK, (r + 1) * ATTN_ROW_CHUNK)
            qb, in_block = divmod(r * ATTN_ROW_CHUNK, BIAS_TILE)
            bias_rows = slice(in_block, in_block + ATTN_ROW_CHUNK)
            blocks = {}
            for c in range(n_blocks):
                gap = 2 if lead is None else qb + lead - c
                if gap < 0:
                    continue
                blk = st.s_ref[rows, c * BIAS_TILE:(c + 1) * BIAS_TILE]
                if gap == 0:
                    blk = blk + st.b0_ref[bias_rows, :]
                elif gap == 1:
                    blk = blk + st.b1_ref[bias_rows, :]
                blocks[c] = blk
            mx = functools.reduce(jnp.maximum, blocks.values())
            m_prev = st.m_ref[rows, :]
            m_new = jnp.maximum(m_prev, jnp.max(mx, axis=1, keepdims=True))
            alpha = jnp.exp2(m_prev - m_new)
            probs = {c: jnp.exp2(blk - m_new) for c, blk in blocks.items()}
            st.m_ref[rows, :] = m_new
            st.a_ref[rows, :] = alpha
            st.l_ref[rows, :] = alpha * st.l_ref[rows, :] + functools.reduce(jnp.add, probs.values())
            for c in range(n_blocks):
                p = probs[c].astype(st.p_ref.dtype) if c in probs else jnp.zeros(
                    (ATTN_ROW_CHUNK, BIAS_TILE), st.p_ref.dtype)
                st.p_ref[rows, c * BIAS_TILE:(c + 1) * BIAS_TILE] = p
    for st in streams:
        pv = jnp.dot(st.p_ref[row0:, :], st.v_ref[pl.ds(off, tk), :], preferred_element_type=jnp.float32)
        for c in range(pv.shape[1] // LANES):
            cols = slice(c * LANES, (c + 1) * LANES)
            st.acc_ref[row0:, cols] = st.a_ref[row0:, :] * st.acc_ref[row0:, cols] + pv[:, cols]


def _attend(streams, i, tq, tk):
    for st in streams:
        st.m_ref[...] = jnp.full_like(st.m_ref, -jnp.inf)
        st.l_ref[...] = jnp.zeros_like(st.l_ref)
        st.acc_ref[...] = jnp.zeros_like(st.acc_ref)

    ratio = tq // tk
    first = ratio * i
    blocks_per_tile = tk // BIAS_TILE

    def far(j, carry):
        _attn_step(streams, j, None, tq, tk)
        return carry

    lax.fori_loop(0, jnp.maximum(first - 1, 0), far, 0)

    @pl.when(i > 0)
    def _():
        _attn_step(streams, first - 1, blocks_per_tile, tq, tk)

    for d in range(ratio):
        @pl.when(first + d >= 0)
        def _(d=d):
            _attn_step(streams, first + d, -d * blocks_per_tile, tq, tk)
    return [st.acc_ref[...] / jnp.sum(st.l_ref[...], axis=1, keepdims=True) for st in streams]


def _moba_attn_kernel(qa_ref, qb_ref, ka_ref, kb_ref, va_ref, vb_ref, b0a_ref, b0b_ref, b1a_ref, b1b_ref,
                      o_ref, *scratch, tq, tk):
    n = len(_STREAM_SCRATCH)
    streams = [_Stream(qa_ref, ka_ref, va_ref, b0a_ref, b1a_ref, *scratch[:n]),
               _Stream(qb_ref, kb_ref, vb_ref, b0b_ref, b1b_ref, *scratch[n:])]
    oa, ob = _attend(streams, pl.program_id(1), tq, tk)
    o_ref[:, :HEAD_DIM] = oa.astype(o_ref.dtype)
    o_ref[:, HEAD_DIM:] = ob.astype(o_ref.dtype)


def _moba_attention(qa, ka, v, v_col0, b0, b1, tq=ATTN_Q_TILE, tk=ATTN_K_TILE):
    s = qa.shape[0]
    aug = 2 * HEAD_DIM

    def pair(spec_of_head):
        return [spec_of_head(0), spec_of_head(1)]

    return pl.pallas_call(
        functools.partial(_moba_attn_kernel, tq=tq, tk=tk),
        grid=(MOBA_HEADS // 2, s // tq),
        in_specs=(pair(lambda d: pl.BlockSpec((tq, aug), lambda g, i: (i, 2 * g + d)))
                  + pair(lambda d: pl.BlockSpec((s, aug), lambda g, i: (0, 2 * g + d)))
                  + pair(lambda d: pl.BlockSpec((s, HEAD_DIM), lambda g, i: (0, v_col0 + 2 * g + d)))
                  + pair(lambda d: pl.BlockSpec((None, BIAS_TILE, BIAS_TILE), lambda g, i: (2 * g + d, 0, 0)))
                  + pair(lambda d: pl.BlockSpec((None, BIAS_TILE, BIAS_TILE), lambda g, i: (2 * g + d, 0, 0)))),
        out_specs=pl.BlockSpec((tq, 2 * HEAD_DIM), lambda g, i: (i, g)),
        out_shape=jax.ShapeDtypeStruct((s, MOBA_WIDTH), jnp.bfloat16),
        scratch_shapes=2 * _stream_scratch(tq, tk, HEAD_DIM),
        compiler_params=_params(("arbitrary", "arbitrary"),
                                (2 * _nbytes((tq, aug), jnp.bfloat16), 2 * _nbytes((s, aug), jnp.bfloat16),
                                 2 * _nbytes((s, HEAD_DIM), jnp.bfloat16),
                                 4 * _nbytes((BIAS_TILE, BIAS_TILE), jnp.float32),
                                 _nbytes((tq, 2 * HEAD_DIM), jnp.bfloat16)),
                                (2 * _stream_scratch_bytes(tq, tk, HEAD_DIM),)),
        name="moba_attention",
    )(qa, qa, ka, ka, v, v, b0, b0, b1, b1)


def _diff_attn_kernel(lam_ref, q1_ref, q2_ref, k1_ref, k2_ref, v_ref, b0_ref, b1_ref, g_ref, o_ref,
                      *scratch, tq, tk, out_scale):
    n = len(_STREAM_SCRATCH)
    streams = [_Stream(q1_ref, k1_ref, v_ref, b0_ref, b1_ref, *scratch[:n]),
               _Stream(q2_ref, k2_ref, v_ref, b0_ref, b1_ref, *scratch[n:])]
    o1, o2 = _attend(streams, pl.program_id(1), tq, tk)
    y = o1 - lam_ref[0, 0] * o2
    y = y * lax.rsqrt(jnp.mean(y * y, axis=1, keepdims=True) + LN_EPS) * g_ref[...] * out_scale
    o_ref[...] = y.astype(o_ref.dtype)


def _diff_attention(lam, proj, q_col0, k_col0, v_col0, b0, b1, subln_g, out_scale,
                    tq=ATTN_Q_TILE, tk=ATTN_K_TILE):
    s = proj.shape[0]
    return pl.pallas_call(
        functools.partial(_diff_attn_kernel, tq=tq, tk=tk, out_scale=out_scale),
        grid=(DIFF_HEADS, s // tq),
        in_specs=[pl.BlockSpec(memory_space=pltpu.SMEM),
                  pl.BlockSpec((tq, HEAD_DIM), lambda h, i: (i, q_col0 + 2 * h)),
                  pl.BlockSpec((tq, HEAD_DIM), lambda h, i: (i, q_col0 + 2 * h + 1)),
                  pl.BlockSpec((s, HEAD_DIM), lambda h, i: (0, k_col0 + 2 * h)),
                  pl.BlockSpec((s, HEAD_DIM), lambda h, i: (0, k_col0 + 2 * h + 1)),
                  pl.BlockSpec((s, DIFF_V_DIM), lambda h, i: (0, v_col0 + h)),
                  pl.BlockSpec((None, BIAS_TILE, BIAS_TILE), lambda h, i: (MOBA_HEADS + h, 0, 0)),
                  pl.BlockSpec((None, BIAS_TILE, BIAS_TILE), lambda h, i: (MOBA_HEADS + h, 0, 0)),
                  pl.BlockSpec((1, DIFF_V_DIM), lambda h, i: (0, 0))],
        out_specs=pl.BlockSpec((tq, DIFF_V_DIM), lambda h, i: (i, h)),
        out_shape=jax.ShapeDtypeStruct((s, DIFF_WIDTH), jnp.bfloat16),
        scratch_shapes=2 * _stream_scratch(tq, tk, DIFF_V_DIM),
        compiler_params=_params(("arbitrary", "arbitrary"),
                                (2 * _nbytes((tq, HEAD_DIM), jnp.bfloat16),
                                 2 * _nbytes((s, HEAD_DIM), jnp.bfloat16),
                                 _nbytes((s, DIFF_V_DIM), jnp.bfloat16),
                                 2 * _nbytes((BIAS_TILE, BIAS_TILE), jnp.float32),
                                 _nbytes((tq, DIFF_V_DIM), jnp.bfloat16)),
                                (2 * _stream_scratch_bytes(tq, tk, DIFF_V_DIM),)),
        name="diff_attention",
    )(lam, proj, proj, proj, proj, proj, b0, b1, subln_g)


def _sigmoid(x):
    return 1.0 / (1.0 + jnp.exp(-x))


def _merge_kernel(ya_ref, yb_ref, wa_ref, wb_ref, ga_ref, gb_ref, ba_ref, bb_ref, o_ref):
    pa = lax.dot_general(ya_ref[...], wa_ref[...], _NN, preferred_element_type=jnp.float32)
    pb = lax.dot_general(yb_ref[...], wb_ref[...], _NN, preferred_element_type=jnp.float32)
    merged = (_sigmoid(ga_ref[...] + ba_ref[...]) * pa + _sigmoid(gb_ref[...] + bb_ref[...]) * pb)
    o_ref[...] = merged.astype(o_ref.dtype)


def _gated_merge(ya, yb, wa, wb, gates, gate_b, tm=512, tn=1024):
    s, k = ya.shape
    n = wa.shape[1]
    nb = n // tn
    return pl.pallas_call(
        _merge_kernel,
        grid=(nb, s // tm),
        in_specs=[pl.BlockSpec((tm, k), lambda j, i: (i, 0)),
                  pl.BlockSpec((tm, k), lambda j, i: (i, 0)),
                  pl.BlockSpec((k, tn), lambda j, i: (0, j)),
                  pl.BlockSpec((k, tn), lambda j, i: (0, j)),
                  pl.BlockSpec((tm, tn), lambda j, i: (i, j)),
                  pl.BlockSpec((tm, tn), lambda j, i: (i, nb + j)),
                  pl.BlockSpec((1, tn), lambda j, i: (0, j)),
                  pl.BlockSpec((1, tn), lambda j, i: (0, nb + j))],
        out_specs=pl.BlockSpec((tm, tn), lambda j, i: (i, j)),
        out_shape=jax.ShapeDtypeStruct((s, n), jnp.bfloat16),
        compiler_params=_params(("arbitrary", "arbitrary"),
                                (2 * _nbytes((tm, k), ya.dtype), 2 * _nbytes((k, tn), wa.dtype),
                                 2 * _nbytes((tm, tn), gates.dtype),
                                 _nbytes((tm, tn), jnp.bfloat16))),
        name="gated_merge",
    )(ya, yb, wa, wb, gates, gates, gate_b, gate_b)


def _residual_ln_kernel(x_ref, h_ref, g_ref, b_ref, *o_refs):
    z = DEEPNORM_ALPHA * x_ref[...] + h_ref[...]
    mu = jnp.mean(z, axis=1, keepdims=True)
    zc = z - mu
    var = jnp.mean(zc * zc, axis=1, keepdims=True)
    y = zc * lax.rsqrt(var + LN_EPS) * g_ref[...] + b_ref[...]
    for o_ref in o_refs:
        o_ref[...] = y.astype(o_ref.dtype)


def _residual_ln(x, h, g, b, out_dtypes, tm=256):
    s, d = x.shape
    row = pl.BlockSpec((tm, d), lambda i: (i, 0))
    vec = pl.BlockSpec((1, d), lambda i: (0, 0))
    return pl.pallas_call(
        _residual_ln_kernel,
        grid=(s // tm,),
        in_specs=[row, row, vec, vec],
        out_specs=[row for _ in out_dtypes],
        out_shape=[jax.ShapeDtypeStruct((s, d), dt) for dt in out_dtypes],
        compiler_params=_params(("arbitrary",),
                                [_nbytes((tm, d), x.dtype), _nbytes((tm, d), h.dtype)]
                                + [_nbytes((tm, d), dt) for dt in out_dtypes]),
        name="residual_layernorm",
    )(x, h, g, b)


def _top_rows_one_by_one(s, k):
    n = s.shape[0]
    ridx = lax.broadcasted_iota(jnp.int32, s.shape, 0).astype(jnp.float32)
    rows, rank = [], jnp.full(s.shape, float(k), jnp.float32)
    for i in range(k):
        mx = jnp.max(s, axis=0, keepdims=True)
        rows.append(mx)
        first = jnp.min(jnp.where(s == mx, ridx, float(n)), axis=0, keepdims=True)
        hit = ridx == first
        rank = jnp.where(hit, float(i), rank)
        s = jnp.where(hit, -jnp.inf, s)
    return tuple(rows) + (rank,)


def _top_rows(s, k):
    rows, rest, rank = [], s, jnp.full(s.shape, float(k), jnp.float32)
    for i in range(k):
        mx = jnp.max(rest, axis=0, keepdims=True)
        rows.append(mx)
        hit = rest == mx
        rank = jnp.where(hit, float(i), rank)
        rest = jnp.where(hit, -jnp.inf, rest)
    removed = jnp.sum(jnp.where(rest == -jnp.inf, 1.0, 0.0), axis=0, keepdims=True)
    out = lax.cond(jnp.max(removed) > k, lambda: _top_rows_one_by_one(s, k),
                   lambda: tuple(rows) + (rank,))
    return out[:k], out[k]


def _stack_rows(rows):
    k, tm = len(rows), rows[0].shape[1]
    ridx = lax.broadcasted_iota(jnp.int32, (k, tm), 0)
    out = jnp.zeros((k, tm), rows[0].dtype)
    for r, row in enumerate(rows):
        out = jnp.where(ridx == r, row, out)
    return out


def _peer_route_kernel(q_ref, k1_ref, k2_ref, c1_ref, e1_ref, r2_ref, e2_ref, *, tm):
    q = q_ref[...]
    s1 = lax.dot_general(k1_ref[...], q[:, :PEER_HALF], _NT, precision=lax.Precision.HIGHEST,
                         preferred_element_type=jnp.float32)
    s2 = lax.dot_general(k2_ref[...], q[:, PEER_HALF:], _NT, precision=lax.Precision.HIGHEST,
                         preferred_element_type=jnp.float32)
    v1, rank1 = _top_rows(s1, PEER_TOPK)
    v2, rank2 = _top_rows(s2, PEER_TOPK)
    v1m = _stack_rows(v1)
    v2m = _stack_rows(v2)
    half = PEER_TOPK // 2
    assert all((a + 1) * (b + 1) > PEER_TOPK for a in range(1, half) for b in range(half, PEER_TOPK))
    assert all((a + 1) * (b + 1) > PEER_TOPK for a in range(half, PEER_TOPK) for b in range(1, PEER_TOPK))
    cand = jnp.concatenate([v1[0] + v2m] + [v1[a] + v2m[:half] for a in range(1, half)]
                           + [v1m[half:] + v2[0]], axis=0)
    top, _ = _top_rows(cand, PEER_TOPK)
    thr = top[PEER_TOPK - 1]
    sums = [v1[a] + v2m for a in range(PEER_TOPK)]
    above = [jnp.sum(jnp.where(sm > thr, 1.0, 0.0), axis=0, keepdims=True) for sm in sums]
    equal = [jnp.sum(jnp.where(sm == thr, 1.0, 0.0), axis=0, keepdims=True) for sm in sums]
    spare = float(PEER_TOPK) - functools.reduce(jnp.add, above)
    bidx = lax.broadcasted_iota(jnp.int32, (PEER_TOPK, tm), 0).astype(jnp.float32)
    c1 = jnp.zeros_like(s1)
    z = jnp.zeros_like(thr)
    for a in range(PEER_TOPK):
        taken = jnp.minimum(equal[a], spare)
        spare = spare - taken
        count = above[a] + taken
        z = z + jnp.sum(jnp.where(bidx < count, jnp.exp(sums[a] - top[0]), 0.0), axis=0, keepdims=True)
        c1 = jnp.where(rank1 == float(a), count, c1)
    c1_ref[...] = c1
    e1_ref[...] = jnp.exp(s1 - v1[0])
    r2_ref[...] = rank2.astype(r2_ref.dtype)
    e2_ref[...] = (jnp.exp(s2 - v2[0]) / z).astype(e2_ref.dtype)


def _peer_route(qp, keys1, keys2, tm=512):
    s = qp.shape[0]
    shape = (PEER_HEADS, PEER_N_KEYS, s)
    spec = pl.BlockSpec((None, PEER_N_KEYS, tm), lambda i, h: (h, 0, i))
    key_spec = pl.BlockSpec((PEER_N_KEYS, PEER_HALF), lambda i, h: (0, 0))
    return pl.pallas_call(
        functools.partial(_peer_route_kernel, tm=tm),
        grid=(s // tm, PEER_HEADS),
        in_specs=[pl.BlockSpec((tm, PEER_QUERY_DIM), lambda i, h: (i, h)), key_spec, key_spec],
        out_specs=[spec, spec, spec, spec],
        out_shape=[jax.ShapeDtypeStruct(shape, jnp.float32), jax.ShapeDtypeStruct(shape, jnp.float32),
                   jax.ShapeDtypeStruct(shape, jnp.bfloat16), jax.ShapeDtypeStruct(shape, jnp.bfloat16)],
        compiler_params=_params(("arbitrary", "arbitrary"),
                                (_nbytes((tm, PEER_QUERY_DIM), jnp.float32),
                                 2 * _nbytes((PEER_N_KEYS, PEER_HALF), jnp.float32),
                                 3 * _nbytes((PEER_N_KEYS, tm), jnp.float32))),
        name="peer_route",
    )(qp, keys1, keys2)


def _gelu(x):
    return 0.5 * x * (1.0 + lax.erf(x * math.sqrt(0.5)))


def _peer_expert_kernel(x_ref, u_ref, v_ref, c1_ref, e1_ref, r2_ref, e2_ref, o_ref, *w_refs, tm, te):
    e = pl.program_id(1)

    @pl.when(e == 0)
    def _():
        o_ref[...] = jnp.zeros_like(o_ref)

    gate_dtype = r2_ref.dtype
    groups = MXU_DIM // PEER_N_KEYS
    outs = []
    for chain, w_ref in enumerate(w_refs):
        base = chain * MXU_DIM
        act = lax.dot_general(u_ref[base:base + MXU_DIM, :], x_ref[...], _NT,
                              preferred_element_type=jnp.float32)
        act = _gelu(act).astype(gate_dtype)
        for c in range(groups):
            i1 = e * (te // PEER_N_KEYS) + chain * groups + c
            g = None
            for h in range(PEER_HEADS):
                c1_row = c1_ref[h, pl.ds(i1, 1), :].astype(gate_dtype)
                e1_row = e1_ref[h, pl.ds(i1, 1), :].astype(gate_dtype)
                term = jnp.where(r2_ref[h] < c1_row, e2_ref[h], 0.0) * e1_row
                g = term if g is None else g + term
            lo = c * PEER_N_KEYS
            w_ref[lo:lo + PEER_N_KEYS, :] = g * act[lo:lo + PEER_N_KEYS, :]
        outs.append(lax.dot_general(w_ref[...], v_ref[base:base + MXU_DIM, :], _TN,
                                    preferred_element_type=jnp.float32))
    o_ref[...] += functools.reduce(jnp.add, outs)


def _peer_experts(x, u, v, c1, e1, r2, e2, tm=512, te=512):
    s, d = x.shape
    n_exp = u.shape[0]
    once = pl.Buffered(1)
    gate_spec = pl.BlockSpec((PEER_HEADS, PEER_N_KEYS, tm), lambda i, e: (0, 0, i), pipeline_mode=once)
    gate_bytes = sum(_nbytes((PEER_HEADS, PEER_N_KEYS, tm), a.dtype) for a in (c1, e1, r2, e2))
    return pl.pallas_call(
        functools.partial(_peer_expert_kernel, tm=tm, te=te),
        grid=(s // tm, n_exp // te),
        in_specs=[pl.BlockSpec((tm, d), lambda i, e: (i, 0), pipeline_mode=once),
                  pl.BlockSpec((te, d), lambda i, e: (e, 0)),
                  pl.BlockSpec((te, d), lambda i, e: (e, 0)),
                  gate_spec, gate_spec, gate_spec, gate_spec],
        out_specs=pl.BlockSpec((tm, d), lambda i, e: (i, 0)),
        out_shape=jax.ShapeDtypeStruct((s, d), jnp.float32),
        scratch_shapes=[pltpu.VMEM((MXU_DIM, tm), r2.dtype) for _ in range(te // MXU_DIM)],
        compiler_params=_params(("arbitrary", "arbitrary"),
                                (2 * _nbytes((te, d), u.dtype), _nbytes((tm, d), jnp.float32)),
                                (_nbytes((tm, d), x.dtype), gate_bytes, _nbytes((te, tm), r2.dtype))),
        name="peer_experts",
    )(x, u, v, c1, e1, r2, e2)


def _layer(x, layer, w_in, gate_b, rel_bias, lq1, lk1, lq2, lk2, subln_g, w_o_moba, w_o_diff, w_out,
           ln1_g, ln1_b, peer_wq, keys1, keys2, peer_u, peer_v, ln2_g, ln2_b):
    bf16 = jnp.bfloat16
    f32 = jnp.float32
    lambda_init = 0.8 - 0.6 * math.exp(-0.3 * layer)
    xb = x.astype(bf16)

    c0 = 2 * MOBA_WIDTH
    c1 = c0 + MOBA_WIDTH + 3 * DIFF_WIDTH
    col_scale = jnp.concatenate([jnp.ones((MOBA_WIDTH,), f32), jnp.full((DIFF_WIDTH,), QK_SCALE_LOG2, f32),
                                 jnp.ones((2 * DIFF_WIDTH,), f32)]).reshape(1, c1 - c0)
    mqk = _matmul(xb, w_in, 0, c0, f32, 512, 1024, "proj_moba_qk")
    mid = _matmul(xb, w_in, c0, c1 - c0, bf16, 512, 1024, "proj_attn", col_scale)
    gates = _matmul(xb, w_in, c1, 2 * D_MODEL, bf16, 512, 1024, "proj_gates")

    b0, b1 = _bias_tiles(rel_bias, BIAS_TILE)
    qa, ka = _moba_route(mqk)
    y_a = _moba_attention(qa, ka, mid, 0, b0, b1)

    lam = (jnp.exp(jnp.sum(lq1.astype(f32) * lk1.astype(f32)))
           - jnp.exp(jnp.sum(lq2.astype(f32) * lk2.astype(f32))) + lambda_init).reshape(1, 1)
    hb = MOBA_WIDTH // HEAD_DIM
    y_b = _diff_attention(lam, mid, hb, hb + DIFF_WIDTH // HEAD_DIM,
                          (MOBA_WIDTH + 2 * DIFF_WIDTH) // DIFF_V_DIM, b0, b1,
                          subln_g.reshape(1, DIFF_V_DIM).astype(f32), 1.0 - lambda_init)

    merged = _gated_merge(y_a, y_b, w_o_moba, w_o_diff, gates,
                          gate_b.reshape(1, 2 * D_MODEL).astype(f32))
    h = _matmul(merged, w_out, 0, D_MODEL, bf16, 512, 1024, "proj_out")
    x1, x1b = _residual_ln(x, h, ln1_g.reshape(1, D_MODEL), ln1_b.reshape(1, D_MODEL), (f32, bf16))

    qp = _matmul(x1b, peer_wq, 0, PEER_HEADS * PEER_QUERY_DIM, f32, 512, 1024, "peer_query")
    c1, e1, r2, e2 = _peer_route(qp, keys1, keys2)
    h2 = _peer_experts(x1b, peer_u.astype(bf16), peer_v.astype(bf16), c1, e1, r2, e2)
    (x2,) = _residual_ln(x1, h2, ln2_g.reshape(1, D_MODEL), ln2_b.reshape(1, D_MODEL), (f32,))
    return x2


def kernel(x, w_in, gate_b, rel_bias, lambda_q1, lambda_k1, lambda_q2, lambda_k2, subln_g, w_o_moba,
           w_o_diff, w_out, ln1_g, ln1_b, peer_wq, peer_keys1, peer_keys2, peer_u, peer_v, ln2_g, ln2_b):
    b, s, d = x.shape
    assert (b, s, d) == (1, SEQ, D_MODEL) and w_in.shape[0] == DEPTH
    xs = x.reshape(s, d)
    for layer in range(DEPTH):
        xs = _layer(xs, layer, w_in[layer], gate_b[layer], rel_bias, lambda_q1[layer], lambda_k1[layer],
                    lambda_q2[layer], lambda_k2[layer], subln_g[layer], w_o_moba[layer], w_o_diff[layer],
                    w_out[layer], ln1_g[layer], ln1_b[layer], peer_wq[layer], peer_keys1[layer],
                    peer_keys2[layer], peer_u[layer], peer_v[layer], ln2_g[layer], ln2_b[layer])
    return xs.reshape(b, s, d)
```

```python
import functools
import math
from typing import Any, NamedTuple

import jax
import jax.numpy as jnp
import numpy as np
from jax import lax
from jax.experimental import pallas as pl
from jax.experimental.pallas import tpu as pltpu

D_MODEL = 4096
SEQ = 8192
DEPTH = 1
HEAD_DIM = 128
MOBA_HEADS = 16
MOBA_WIDTH = MOBA_HEADS * HEAD_DIM
MOBA_BLOCK = 256
MOBA_TOPK = 3
N_MOBA_BLOCKS = SEQ // MOBA_BLOCK
DIFF_HEADS = 8
DIFF_V_DIM = 2 * HEAD_DIM
DIFF_WIDTH = DIFF_HEADS * DIFF_V_DIM
N_BUCKETS = 32
MAX_DISTANCE = 128
PEER_HEADS = 8
PEER_N_KEYS = 128
PEER_HALF = 128
PEER_QUERY_DIM = 2 * PEER_HALF
PEER_TOPK = 16
DEEPNORM_ALPHA = (2.0 * DEPTH) ** 0.25
LN_EPS = 1e-5
NEG_INF = -1e30

LOG2E = math.log2(math.e)
QK_SCALE_LOG2 = HEAD_DIM ** -0.5 * LOG2E

LANES = 128
MXU_DIM = 256
VMEM_LIMIT_CAP = 60000 * 1024
COMPILER_SCRATCH_BYTES = 8 << 20

ATTN_Q_TILE = 1024
ATTN_K_TILE = 512
BIAS_TILE = LANES
ATTN_ROW_CHUNK = 32

_NT = (((1,), (1,)), ((), ()))
_TN = (((0,), (0,)), ((), ()))


def _nbytes(shape, dtype):
    return int(np.prod(shape)) * jnp.dtype(dtype).itemsize


def _params(semantics, pipelined=(), resident=()):
    need = 2 * sum(pipelined) + sum(resident) + COMPILER_SCRATCH_BYTES
    return pltpu.CompilerParams(dimension_semantics=semantics,
                                vmem_limit_bytes=min(need, VMEM_LIMIT_CAP))


_NN = (((1,), (0,)), ((), ()))


def _matmul_kernel(a_ref, b_ref, *rest):
    *scale, o_ref = rest
    acc = lax.dot_general(a_ref[...], b_ref[...], _NN, preferred_element_type=jnp.float32)
    for s_ref in scale:
        acc = acc * s_ref[...]
    o_ref[...] = acc.astype(o_ref.dtype)


def _matmul(a, w, col0, n, out_dtype, tm, tn, name, col_scale=None):
    m, k = a.shape
    c0 = col0 // tn
    scale_specs = [] if col_scale is None else [pl.BlockSpec((1, tn), lambda j, i: (0, j))]
    scale_args = [] if col_scale is None else [col_scale]
    return pl.pallas_call(
        _matmul_kernel,
        grid=(n // tn, m // tm),
        in_specs=[pl.BlockSpec((tm, k), lambda j, i: (i, 0)),
                  pl.BlockSpec((k, tn), lambda j, i: (0, c0 + j))] + scale_specs,
        out_specs=pl.BlockSpec((tm, tn), lambda j, i: (i, j)),
        out_shape=jax.ShapeDtypeStruct((m, n), out_dtype),
        compiler_params=_params(("arbitrary", "arbitrary"),
                                (_nbytes((tm, k), a.dtype), _nbytes((k, tn), w.dtype),
                                 _nbytes((tm, tn), out_dtype))),
        name=name,
    )(a, w, *scale_args)


def _moba_route_kernel(q_ref, k_ref, qa_ref, ka_ref, kmean_ref, *, tq):
    i = pl.program_id(1)

    @pl.when(i == 0)
    def _():
        k3 = k_ref[...].reshape(N_MOBA_BLOCKS, MOBA_BLOCK, HEAD_DIM)
        kmean_ref[...] = jnp.zeros_like(kmean_ref)
        kmean_ref[0:N_MOBA_BLOCKS, :] = jnp.sum(k3, axis=1) * (1.0 / MOBA_BLOCK)

    q = q_ref[...]
    shift = int(math.log2(MOBA_BLOCK))
    gate = lax.dot_general(kmean_ref[0:N_MOBA_BLOCKS, :], q, _NT, precision=lax.Precision.HIGHEST,
                           preferred_element_type=jnp.float32)
    blk = lax.broadcasted_iota(jnp.int32, (N_MOBA_BLOCKS, tq), 0)
    own_t = jnp.right_shift(i * tq + lax.broadcasted_iota(jnp.int32, (N_MOBA_BLOCKS, tq), 1), shift)
    blk_f = blk.astype(jnp.float32)
    past = blk < own_t
    g = jnp.where(past, gate, NEG_INF)
    picked = jnp.zeros((N_MOBA_BLOCKS, tq), jnp.float32)
    for _ in range(MOBA_TOPK):
        mx = jnp.max(g, axis=0, keepdims=True)
        first = jnp.min(jnp.where(g == mx, blk_f, float(N_MOBA_BLOCKS)), axis=0, keepdims=True)
        hit = blk_f == first
        picked = jnp.where(hit, 1.0, picked)
        g = jnp.where(hit, -jnp.inf, g)
    allowed = jnp.where(past, picked, 0.0)
    allowed = jnp.where(blk == own_t, 1.0, allowed)
    blocked = jnp.concatenate([1.0 - allowed, jnp.zeros((LANES - N_MOBA_BLOCKS, tq), jnp.float32)],
                              axis=0).T
    lane = lax.broadcasted_iota(jnp.int32, (tq, LANES), 1)
    own = jnp.right_shift(i * tq + lax.broadcasted_iota(jnp.int32, (tq, LANES), 0), shift)
    qa_ref[:, :HEAD_DIM] = (q * QK_SCALE_LOG2).astype(qa_ref.dtype)
    qa_ref[:, HEAD_DIM:] = blocked.astype(qa_ref.dtype)
    kb = k_ref[pl.ds(pl.multiple_of(i * tq, tq), tq), :]
    ka_ref[:, :HEAD_DIM] = kb.astype(ka_ref.dtype)
    ka_ref[:, HEAD_DIM:] = jnp.where(lane == own, NEG_INF, 0.0).astype(ka_ref.dtype)


def _moba_route(mqk, tq=1024):
    s = mqk.shape[0]
    aug = 2 * HEAD_DIM
    out = jax.ShapeDtypeStruct((s, MOBA_HEADS * aug), jnp.bfloat16)
    return pl.pallas_call(
        functools.partial(_moba_route_kernel, tq=tq),
        grid=(MOBA_HEADS, s // tq),
        in_specs=[pl.BlockSpec((tq, HEAD_DIM), lambda h, i: (i, h)),
                  pl.BlockSpec((s, HEAD_DIM), lambda h, i: (0, MOBA_HEADS + h))],
        out_specs=[pl.BlockSpec((tq, aug), lambda h, i: (i, h)),
                   pl.BlockSpec((tq, aug), lambda h, i: (i, h))],
        out_shape=[out, out],
        scratch_shapes=[pltpu.VMEM((LANES, HEAD_DIM), jnp.float32)],
        compiler_params=_params(("arbitrary", "arbitrary"),
                                (_nbytes((tq, HEAD_DIM), jnp.float32),
                                 _nbytes((s, HEAD_DIM), jnp.float32),
                                 2 * _nbytes((tq, aug), jnp.bfloat16))),
        name="moba_route",
    )(mqk, mqk)


def _t5_bucket(dist, xp):
    n = xp.maximum(dist, 0)
    max_exact = N_BUCKETS // 2
    nf = xp.maximum(n, 1).astype(xp.float32)
    large = max_exact + (xp.log(nf / max_exact) / math.log(MAX_DISTANCE / max_exact)
                         * (N_BUCKETS - max_exact)).astype(xp.int32)
    large = xp.minimum(large, N_BUCKETS - 1)
    return xp.where(n < max_exact, n, large)


def _toeplitz(g, t):
    heads = g.shape[0]
    rows = jnp.tile(g, (1, t))[:, :t * (2 * t - 1)].reshape(heads, t, 2 * t - 1)
    return rows[:, :, :t]


def _bias_tiles(rel_bias, t):
    assert int(_t5_bucket(np.array([t + 1]), np)[0]) == N_BUCKETS - 1
    tab = rel_bias.T.astype(jnp.float32)
    rel = (tab - tab[:, N_BUCKETS - 1:]) * LOG2E
    c = jnp.arange(2 * t, dtype=jnp.int32)
    d0 = jnp.where(c == 0, 0, 2 * t - c)
    g0 = jnp.where((c == 0) | (c > t), rel[:, _t5_bucket(d0, jnp)], NEG_INF)
    d1 = jnp.where(c < t, t - c, 3 * t - c)
    g1 = rel[:, _t5_bucket(d1, jnp)]
    return _toeplitz(g0, t), _toeplitz(g1, t)


class _Stream(NamedTuple):
    q_ref: Any
    k_ref: Any
    v_ref: Any
    b0_ref: Any
    b1_ref: Any
    s_ref: Any
    p_ref: Any
    m_ref: Any
    a_ref: Any
    l_ref: Any
    acc_ref: Any


_STREAM_SCRATCH = ("s_ref", "p_ref", "m_ref", "a_ref", "l_ref", "acc_ref")


def _stream_scratch(tq, tk, dv):
    return [pltpu.VMEM((tq, tk), jnp.float32), pltpu.VMEM((tq, tk), jnp.bfloat16),
            pltpu.VMEM((tq, LANES), jnp.float32), pltpu.VMEM((tq, LANES), jnp.float32),
            pltpu.VMEM((tq, LANES), jnp.float32), pltpu.VMEM((tq, dv), jnp.float32)]


def _stream_scratch_bytes(tq, tk, dv):
    return (_nbytes((tq, tk), jnp.float32) + _nbytes((tq, tk), jnp.bfloat16)
            + 3 * _nbytes((tq, LANES), jnp.float32) + _nbytes((tq, dv), jnp.float32))


def _attn_step(streams, j, lead, tq, tk):
    off = pl.multiple_of(j * tk, tk)
    n_blocks = tk // BIAS_TILE
    row0 = 0 if lead is None else max(0, -lead) * BIAS_TILE
    for st in streams:
        st.s_ref[row0:, :] = lax.dot_general(st.q_ref[row0:, :], st.k_ref[pl.ds(off, tk), :], _NT,
                                             preferred_element_type=jnp.float32)
    for st in streams:
        for r in range(row0 // ATTN_ROW_CHUNK, tq // ATTN_ROW_CHUNK):
            rows = slice(r * ATTN_ROW_CHUNK, (r + 1) * ATTN_ROW_CHUNK)
            qb, in_block = divmod(r * ATTN_ROW_CHUNK, BIAS_TILE)
            bias_rows = slice(in_block, in_block + ATTN_ROW_CHUNK)
            blocks = {}
            for c in range(n_blocks):
                gap = 2 if lead is None else qb + lead - c
                if gap < 0:
                    continue
                blk = st.s_ref[rows, c * BIAS_TILE:(c + 1) * BIAS_TILE]
                if gap == 0:
                    blk = blk + st.b0_ref[bias_rows, :]
                elif gap == 1:
                    blk = blk + st.b1_ref[bias_rows, :]
                blocks[c] = blk
            mx = functools.reduce(jnp.maximum, blocks.values())
            m_prev = st.m_ref[rows, :]
            m_new = jnp.maximum(m_prev, jnp.max(mx, axis=1, keepdims=True))
            alpha = jnp.exp2(m_prev - m_new)
            probs = {c: jnp.exp2(blk - m_new) for c, blk in blocks.items()}
            st.m_ref[rows, :] = m_new
            st.a_ref[rows, :] = alpha
            st.l_ref[rows, :] = alpha * st.l_ref[rows, :] + functools.reduce(jnp.add, probs.values())
            for c in range(n_blocks):
                p = probs[c].astype(st.p_ref.dtype) if c in probs else jnp.zeros(
                    (ATTN_ROW_CHUNK, BIAS_TILE), st.p_ref.dtype)
                st.p_ref[rows, c * BIAS_TILE:(c + 1) * BIAS_TILE] = p
    for st in streams:
        pv = jnp.dot(st.p_ref[row0:, :], st.v_ref[pl.ds(off, tk), :], preferred_element_type=jnp.float32)
        for c in range(pv.shape[1] // LANES):
            cols = slice(c * LANES, (c + 1) * LANES)
            st.acc_ref[row0:, cols] = st.a_ref[row0:, :] * st.acc_ref[row0:, cols] + pv[:, cols]


def _attend(streams, i, tq, tk):
    for st in streams:
        st.m_ref[...] = jnp.full_like(st.m_ref, -jnp.inf)
        st.l_ref[...] = jnp.zeros_like(st.l_ref)
        st.acc_ref[...] = jnp.zeros_like(st.acc_ref)

    ratio = tq // tk
    first = ratio * i
    blocks_per_tile = tk // BIAS_TILE

    def far(j, carry):
        _attn_step(streams, j, None, tq, tk)
        return carry

    lax.fori_loop(0, jnp.maximum(first - 1, 0), far, 0)

    @pl.when(i > 0)
    def _():
        _attn_step(streams, first - 1, blocks_per_tile, tq, tk)

    for d in range(ratio):
        @pl.when(first + d >= 0)
        def _(d=d):
            _attn_step(streams, first + d, -d * blocks_per_tile, tq, tk)
    return [st.acc_ref[...] / jnp.sum(st.l_ref[...], axis=1, keepdims=True) for st in streams]


def _moba_attn_kernel(qa_ref, qb_ref, ka_ref, kb_ref, va_ref, vb_ref, b0a_ref, b0b_ref, b1a_ref, b1b_ref,
                      o_ref, *scratch, tq, tk):
    n = len(_STREAM_SCRATCH)
    streams = [_Stream(qa_ref, ka_ref, va_ref, b0a_ref, b1a_ref, *scratch[:n]),
               _Stream(qb_ref, kb_ref, vb_ref, b0b_ref, b1b_ref, *scratch[n:])]
    oa, ob = _attend(streams, pl.program_id(1), tq, tk)
    o_ref[:, :HEAD_DIM] = oa.astype(o_ref.dtype)
    o_ref[:, HEAD_DIM:] = ob.astype(o_ref.dtype)


def _moba_attention(qa, ka, v, v_col0, b0, b1, tq=ATTN_Q_TILE, tk=ATTN_K_TILE):
    s = qa.shape[0]
    aug = 2 * HEAD_DIM

    def pair(spec_of_head):
        return [spec_of_head(0), spec_of_head(1)]

    return pl.pallas_call(
        functools.partial(_moba_attn_kernel, tq=tq, tk=tk),
        grid=(MOBA_HEADS // 2, s // tq),
        in_specs=(pair(lambda d: pl.BlockSpec((tq, aug), lambda g, i: (i, 2 * g + d)))
                  + pair(lambda d: pl.BlockSpec((s, aug), lambda g, i: (0, 2 * g + d)))
                  + pair(lambda d: pl.BlockSpec((s, HEAD_DIM), lambda g, i: (0, v_col0 + 2 * g + d)))
                  + pair(lambda d: pl.BlockSpec((None, BIAS_TILE, BIAS_TILE), lambda g, i: (2 * g + d, 0, 0)))
                  + pair(lambda d: pl.BlockSpec((None, BIAS_TILE, BIAS_TILE), lambda g, i: (2 * g + d, 0, 0)))),
        out_specs=pl.BlockSpec((tq, 2 * HEAD_DIM), lambda g, i: (i, g)),
        out_shape=jax.ShapeDtypeStruct((s, MOBA_WIDTH), jnp.bfloat16),
        scratch_shapes=2 * _stream_scratch(tq, tk, HEAD_DIM),
        compiler_params=_params(("arbitrary", "arbitrary"),
                                (2 * _nbytes((tq, aug), jnp.bfloat16), 2 * _nbytes((s, aug), jnp.bfloat16),
                                 2 * _nbytes((s, HEAD_DIM), jnp.bfloat16),
                                 4 * _nbytes((BIAS_TILE, BIAS_TILE), jnp.float32),
                                 _nbytes((tq, 2 * HEAD_DIM), jnp.bfloat16)),
                                (2 * _stream_scratch_bytes(tq, tk, HEAD_DIM),)),
        name="moba_attention",
    )(qa, qa, ka, ka, v, v, b0, b0, b1, b1)


def _diff_attn_kernel(lam_ref, q1_ref, q2_ref, k1_ref, k2_ref, v_ref, b0_ref, b1_ref, g_ref, o_ref,
                      *scratch, tq, tk, out_scale):
    n = len(_STREAM_SCRATCH)
    streams = [_Stream(q1_ref, k1_ref, v_ref, b0_ref, b1_ref, *scratch[:n]),
               _Stream(q2_ref, k2_ref, v_ref, b0_ref, b1_ref, *scratch[n:])]
    o1, o2 = _attend(streams, pl.program_id(1), tq, tk)
    y = o1 - lam_ref[0, 0] * o2
    y = y * lax.rsqrt(jnp.mean(y * y, axis=1, keepdims=True) + LN_EPS) * g_ref[...] * out_scale
    o_ref[...] = y.astype(o_ref.dtype)


def _diff_attention(lam, proj, q_col0, k_col0, v_col0, b0, b1, subln_g, out_scale,
                    tq=ATTN_Q_TILE, tk=ATTN_K_TILE):
    s = proj.shape[0]
    return pl.pallas_call(
        functools.partial(_diff_attn_kernel, tq=tq, tk=tk, out_scale=out_scale),
        grid=(DIFF_HEADS, s // tq),
        in_specs=[pl.BlockSpec(memory_space=pltpu.SMEM),
                  pl.BlockSpec((tq, HEAD_DIM), lambda h, i: (i, q_col0 + 2 * h)),
                  pl.BlockSpec((tq, HEAD_DIM), lambda h, i: (i, q_col0 + 2 * h + 1)),
                  pl.BlockSpec((s, HEAD_DIM), lambda h, i: (0, k_col0 + 2 * h)),
                  pl.BlockSpec((s, HEAD_DIM), lambda h, i: (0, k_col0 + 2 * h + 1)),
                  pl.BlockSpec((s, DIFF_V_DIM), lambda h, i: (0, v_col0 + h)),
                  pl.BlockSpec((None, BIAS_TILE, BIAS_TILE), lambda h, i: (MOBA_HEADS + h, 0, 0)),
                  pl.BlockSpec((None, BIAS_TILE, BIAS_TILE), lambda h, i: (MOBA_HEADS + h, 0, 0)),
                  pl.BlockSpec((1, DIFF_V_DIM), lambda h, i: (0, 0))],
        out_specs=pl.BlockSpec((tq, DIFF_V_DIM), lambda h, i: (i, h)),
        out_shape=jax.ShapeDtypeStruct((s, DIFF_WIDTH), jnp.bfloat16),
        scratch_shapes=2 * _stream_scratch(tq, tk, DIFF_V_DIM),
        compiler_params=_params(("arbitrary", "arbitrary"),
                                (2 * _nbytes((tq, HEAD_DIM), jnp.bfloat16),
                                 2 * _nbytes((s, HEAD_DIM), jnp.bfloat16),
                                 _nbytes((s, DIFF_V_DIM), jnp.bfloat16),
                                 2 * _nbytes((BIAS_TILE, BIAS_TILE), jnp.float32),
                                 _nbytes((tq, DIFF_V_DIM), jnp.bfloat16)),
                                (2 * _stream_scratch_bytes(tq, tk, DIFF_V_DIM),)),
        name="diff_attention",
    )(lam, proj, proj, proj, proj, proj, b0, b1, subln_g)


def _sigmoid(x):
    return 1.0 / (1.0 + jnp.exp(-x))


def _merge_kernel(ya_ref, yb_ref, wa_ref, wb_ref, ga_ref, gb_ref, ba_ref, bb_ref, o_ref):
    pa = lax.dot_general(ya_ref[...], wa_ref[...], _NN, preferred_element_type=jnp.float32)
    pb = lax.dot_general(yb_ref[...], wb_ref[...], _NN, preferred_element_type=jnp.float32)
    merged = (_sigmoid(ga_ref[...] + ba_ref[...]) * pa + _sigmoid(gb_ref[...] + bb_ref[...]) * pb)
    o_ref[...] = merged.astype(o_ref.dtype)


def _gated_merge(ya, yb, wa, wb, gates, gate_b, tm=512, tn=1024):
    s, k = ya.shape
    n = wa.shape[1]
    nb = n // tn
    return pl.pallas_call(
        _merge_kernel,
        grid=(nb, s // tm),
        in_specs=[pl.BlockSpec((tm, k), lambda j, i: (i, 0)),
                  pl.BlockSpec((tm, k), lambda j, i: (i, 0)),
                  pl.BlockSpec((k, tn), lambda j, i: (0, j)),
                  pl.BlockSpec((k, tn), lambda j, i: (0, j)),
                  pl.BlockSpec((tm, tn), lambda j, i: (i, j)),
                  pl.BlockSpec((tm, tn), lambda j, i: (i, nb + j)),
                  pl.BlockSpec((1, tn), lambda j, i: (0, j)),
                  pl.BlockSpec((1, tn), lambda j, i: (0, nb + j))],
        out_specs=pl.BlockSpec((tm, tn), lambda j, i: (i, j)),
        out_shape=jax.ShapeDtypeStruct((s, n), jnp.bfloat16),
        compiler_params=_params(("arbitrary", "arbitrary"),
                                (2 * _nbytes((tm, k), ya.dtype), 2 * _nbytes((k, tn), wa.dtype),
                                 2 * _nbytes((tm, tn), gates.dtype),
                                 _nbytes((tm, tn), jnp.bfloat16))),
        name="gated_merge",
    )(ya, yb, wa, wb, gates, gates, gate_b, gate_b)


def _residual_ln_kernel(x_ref, h_ref, g_ref, b_ref, *o_refs):
    z = DEEPNORM_ALPHA * x_ref[...] + h_ref[...]
    mu = jnp.mean(z, axis=1, keepdims=True)
    zc = z - mu
    var = jnp.mean(zc * zc, axis=1, keepdims=True)
    y = zc * lax.rsqrt(var + LN_EPS) * g_ref[...] + b_ref[...]
    for o_ref in o_refs:
        o_ref[...] = y.astype(o_ref.dtype)


def _residual_ln(x, h, g, b, out_dtypes, tm=256):
    s, d = x.shape
    row = pl.BlockSpec((tm, d), lambda i: (i, 0))
    vec = pl.BlockSpec((1, d), lambda i: (0, 0))
    return pl.pallas_call(
        _residual_ln_kernel,
        grid=(s // tm,),
        in_specs=[row, row, vec, vec],
        out_specs=[row for _ in out_dtypes],
        out_shape=[jax.ShapeDtypeStruct((s, d), dt) for dt in out_dtypes],
        compiler_params=_params(("arbitrary",),
                                [_nbytes((tm, d), x.dtype), _nbytes((tm, d), h.dtype)]
                                + [_nbytes((tm, d), dt) for dt in out_dtypes]),
        name="residual_layernorm",
    )(x, h, g, b)


def _top_rows_one_by_one(s, k):
    n = s.shape[0]
    ridx = lax.broadcasted_iota(jnp.int32, s.shape, 0).astype(jnp.float32)
    rows, rank = [], jnp.full(s.shape, float(k), jnp.float32)
    for i in range(k):
        mx = jnp.max(s, axis=0, keepdims=True)
        rows.append(mx)
        first = jnp.min(jnp.where(s == mx, ridx, float(n)), axis=0, keepdims=True)
        hit = ridx == first
        rank = jnp.where(hit, float(i), rank)
        s = jnp.where(hit, -jnp.inf, s)
    return tuple(rows) + (rank,)


def _top_rows(s, k):
    rows, rest, rank = [], s, jnp.full(s.shape, float(k), jnp.float32)
    for i in range(k):
        mx = jnp.max(rest, axis=0, keepdims=True)
        rows.append(mx)
        hit = rest == mx
        rank = jnp.where(hit, float(i), rank)
        rest = jnp.where(hit, -jnp.inf, rest)
    removed = jnp.sum(jnp.where(rest == -jnp.inf, 1.0, 0.0), axis=0, keepdims=True)
    out = lax.cond(jnp.max(removed) > k, lambda: _top_rows_one_by_one(s, k),
                   lambda: tuple(rows) + (rank,))
    return out[:k], out[k]


def _stack_rows(rows):
    k, tm = len(rows), rows[0].shape[1]
    ridx = lax.broadcasted_iota(jnp.int32, (k, tm), 0)
    out = jnp.zeros((k, tm), rows[0].dtype)
    for r, row in enumerate(rows):
        out = jnp.where(ridx == r, row, out)
    return out


def _peer_route_kernel(q_ref, k1_ref, k2_ref, c1_ref, e1_ref, r2_ref, e2_ref, *, tm):
    q = q_ref[...]
    s1 = lax.dot_general(k1_ref[...], q[:, :PEER_HALF], _NT, precision=lax.Precision.HIGHEST,
                         preferred_element_type=jnp.float32)
    s2 = lax.dot_general(k2_ref[...], q[:, PEER_HALF:], _NT, precision=lax.Precision.HIGHEST,
                         preferred_element_type=jnp.float32)
    v1, rank1 = _top_rows(s1, PEER_TOPK)
    v2, rank2 = _top_rows(s2, PEER_TOPK)
    v1m = _stack_rows(v1)
    v2m = _stack_rows(v2)
    half = PEER_TOPK // 2
    assert all((a + 1) * (b + 1) > PEER_TOPK for a in range(1, half) for b in range(half, PEER_TOPK))
    assert all((a + 1) * (b + 1) > PEER_TOPK for a in range(half, PEER_TOPK) for b in range(1, PEER_TOPK))
    cand = jnp.concatenate([v1[0] + v2m] + [v1[a] + v2m[:half] for a in range(1, half)]
                           + [v1m[half:] + v2[0]], axis=0)
    top, _ = _top_rows(cand, PEER_TOPK)
    thr = top[PEER_TOPK - 1]
    sums = [v1[a] + v2m for a in range(PEER_TOPK)]
    above = [jnp.sum(jnp.where(sm > thr, 1.0, 0.0), axis=0, keepdims=True) for sm in sums]
    equal = [jnp.sum(jnp.where(sm == thr, 1.0, 0.0), axis=0, keepdims=True) for sm in sums]
    spare = float(PEER_TOPK) - functools.reduce(jnp.add, above)
    bidx = lax.broadcasted_iota(jnp.int32, (PEER_TOPK, tm), 0).astype(jnp.float32)
    c1 = jnp.zeros_like(s1)
    z = jnp.zeros_like(thr)
    for a in range(PEER_TOPK):
        taken = jnp.minimum(equal[a], spare)
        spare = spare - taken
        count = above[a] + taken
        z = z + jnp.sum(jnp.where(bidx < count, jnp.exp(sums[a] - top[0]), 0.0), axis=0, keepdims=True)
        c1 = jnp.where(rank1 == float(a), count, c1)
    c1_ref[...] = c1
    e1_ref[...] = jnp.exp(s1 - v1[0])
    r2_ref[...] = rank2.astype(r2_ref.dtype)
    e2_ref[...] = (jnp.exp(s2 - v2[0]) / z).astype(e2_ref.dtype)


def _peer_route(qp, keys1, keys2, tm=512):
    s = qp.shape[0]
    shape = (PEER_HEADS, PEER_N_KEYS, s)
    spec = pl.BlockSpec((None, PEER_N_KEYS, tm), lambda i, h: (h, 0, i))
    key_spec = pl.BlockSpec((PEER_N_KEYS, PEER_HALF), lambda i, h: (0, 0))
    return pl.pallas_call(
        functools.partial(_peer_route_kernel, tm=tm),
        grid=(s // tm, PEER_HEADS),
        in_specs=[pl.BlockSpec((tm, PEER_QUERY_DIM), lambda i, h: (i, h)), key_spec, key_spec],
        out_specs=[spec, spec, spec, spec],
        out_shape=[jax.ShapeDtypeStruct(shape, jnp.float32), jax.ShapeDtypeStruct(shape, jnp.float32),
                   jax.ShapeDtypeStruct(shape, jnp.bfloat16), jax.ShapeDtypeStruct(shape, jnp.bfloat16)],
        compiler_params=_params(("arbitrary", "arbitrary"),
                                (_nbytes((tm, PEER_QUERY_DIM), jnp.float32),
                                 2 * _nbytes((PEER_N_KEYS, PEER_HALF), jnp.float32),
                                 3 * _nbytes((PEER_N_KEYS, tm), jnp.float32))),
        name="peer_route",
    )(qp, keys1, keys2)


def _gelu(x):
    return 0.5 * x * (1.0 + lax.erf(x * math.sqrt(0.5)))


def _peer_expert_kernel(x_ref, u_ref, v_ref, c1_ref, e1_ref, r2_ref, e2_ref, o_ref, *w_refs, tm, te):
    e = pl.program_id(1)

    @pl.when(e == 0)
    def _():
        o_ref[...] = jnp.zeros_like(o_ref)

    gate_dtype = r2_ref.dtype
    groups = MXU_DIM // PEER_N_KEYS
    outs = []
    for chain, w_ref in enumerate(w_refs):
        base = chain * MXU_DIM
        act = lax.dot_general(u_ref[base:base + MXU_DIM, :], x_ref[...], _NT,
                              preferred_element_type=jnp.float32)
        act = _gelu(act).astype(gate_dtype)
        for c in range(groups):
            i1 = e * (te // PEER_N_KEYS) + chain * groups + c
            g = None
            for h in range(PEER_HEADS):
                c1_row = c1_ref[h, pl.ds(i1, 1), :].astype(gate_dtype)
                e1_row = e1_ref[h, pl.ds(i1, 1), :].astype(gate_dtype)
                term = jnp.where(r2_ref[h] < c1_row, e2_ref[h], 0.0) * e1_row
                g = term if g is None else g + term
            lo = c * PEER_N_KEYS
            w_ref[lo:lo + PEER_N_KEYS, :] = g * act[lo:lo + PEER_N_KEYS, :]
        outs.append(lax.dot_general(w_ref[...], v_ref[base:base + MXU_DIM, :], _TN,
                                    preferred_element_type=jnp.float32))
    o_ref[...] += functools.reduce(jnp.add, outs)


def _peer_experts(x, u, v, c1, e1, r2, e2, tm=512, te=512):
    s, d = x.shape
    n_exp = u.shape[0]
    once = pl.Buffered(1)
    gate_spec = pl.BlockSpec((PEER_HEADS, PEER_N_KEYS, tm), lambda i, e: (0, 0, i), pipeline_mode=once)
    gate_bytes = sum(_nbytes((PEER_HEADS, PEER_N_KEYS, tm), a.dtype) for a in (c1, e1, r2, e2))
    return pl.pallas_call(
        functools.partial(_peer_expert_kernel, tm=tm, te=te),
        grid=(s // tm, n_exp // te),
        in_specs=[pl.BlockSpec((tm, d), lambda i, e: (i, 0), pipeline_mode=once),
                  pl.BlockSpec((te, d), lambda i, e: (e, 0)),
                  pl.BlockSpec((te, d), lambda i, e: (e, 0)),
                  gate_spec, gate_spec, gate_spec, gate_spec],
        out_specs=pl.BlockSpec((tm, d), lambda i, e: (i, 0)),
        out_shape=jax.ShapeDtypeStruct((s, d), jnp.float32),
        scratch_shapes=[pltpu.VMEM((MXU_DIM, tm), r2.dtype) for _ in range(te // MXU_DIM)],
        compiler_params=_params(("arbitrary", "arbitrary"),
                                (2 * _nbytes((te, d), u.dtype), _nbytes((tm, d), jnp.float32)),
                                (_nbytes((tm, d), x.dtype), gate_bytes, _nbytes((te, tm), r2.dtype))),
        name="peer_experts",
    )(x, u, v, c1, e1, r2, e2)


def _layer(x, layer, w_in, gate_b, rel_bias, lq1, lk1, lq2, lk2, subln_g, w_o_moba, w_o_diff, w_out,
           ln1_g, ln1_b, peer_wq, keys1, keys2, peer_u, peer_v, ln2_g, ln2_b):
    bf16 = jnp.bfloat16
    f32 = jnp.float32
    lambda_init = 0.8 - 0.6 * math.exp(-0.3 * layer)
    xb = x.astype(bf16)

    c0 = 2 * MOBA_WIDTH
    c1 = c0 + MOBA_WIDTH + 3 * DIFF_WIDTH
    col_scale = jnp.concatenate([jnp.ones((MOBA_WIDTH,), f32), jnp.full((DIFF_WIDTH,), QK_SCALE_LOG2, f32),
                                 jnp.ones((2 * DIFF_WIDTH,), f32)]).reshape(1, c1 - c0)
    mqk = _matmul(xb, w_in, 0, c0, f32, 512, 1024, "proj_moba_qk")
    mid = _matmul(xb, w_in, c0, c1 - c0, bf16, 512, 1024, "proj_attn", col_scale)
    gates = _matmul(xb, w_in, c1, 2 * D_MODEL, bf16, 512, 1024, "proj_gates")

    b0, b1 = _bias_tiles(rel_bias, BIAS_TILE)
    qa, ka = _moba_route(mqk)
    y_a = _moba_attention(qa, ka, mid, 0, b0, b1)

    lam = (jnp.exp(jnp.sum(lq1.astype(f32) * lk1.astype(f32)))
           - jnp.exp(jnp.sum(lq2.astype(f32) * lk2.astype(f32))) + lambda_init).reshape(1, 1)
    hb = MOBA_WIDTH // HEAD_DIM
    y_b = _diff_attention(lam, mid, hb, hb + DIFF_WIDTH // HEAD_DIM,
                          (MOBA_WIDTH + 2 * DIFF_WIDTH) // DIFF_V_DIM, b0, b1,
                          subln_g.reshape(1, DIFF_V_DIM).astype(f32), 1.0 - lambda_init)

    merged = _gated_merge(y_a, y_b, w_o_moba, w_o_diff, gates,
                          gate_b.reshape(1, 2 * D_MODEL).astype(f32))
    h = _matmul(merged, w_out, 0, D_MODEL, bf16, 512, 1024, "proj_out")
    x1, x1b = _residual_ln(x, h, ln1_g.reshape(1, D_MODEL), ln1_b.reshape(1, D_MODEL), (f32, bf16))

    qp = _matmul(x1b, peer_wq, 0, PEER_HEADS * PEER_QUERY_DIM, f32, 512, 1024, "peer_query")
    c1, e1, r2, e2 = _peer_route(qp, keys1, keys2)
    h2 = _peer_experts(x1b, peer_u.astype(bf16), peer_v.astype(bf16), c1, e1, r2, e2)
    (x2,) = _residual_ln(x1, h2, ln2_g.reshape(1, D_MODEL), ln2_b.reshape(1, D_MODEL), (f32,))
    return x2


def kernel(x, w_in, gate_b, rel_bias, lambda_q1, lambda_k1, lambda_q2, lambda_k2, subln_g, w_o_moba,
           w_o_diff, w_out, ln1_g, ln1_b, peer_wq, peer_keys1, peer_keys2, peer_u, peer_v, ln2_g, ln2_b):
    b, s, d = x.shape
    assert (b, s, d) == (1, SEQ, D_MODEL) and w_in.shape[0] == DEPTH
    xs = x.reshape(s, d)
    for layer in range(DEPTH):
        xs = _layer(xs, layer, w_in[layer], gate_b[layer], rel_bias, lambda_q1[layer], lambda_k1[layer],
                    lambda_q2[layer], lambda_k2[layer], subln_g[layer], w_o_moba[layer], w_o_diff[layer],
                    w_out[layer], ln1_g[layer], ln1_b[layer], peer_wq[layer], peer_keys1[layer],
                    peer_keys2[layer], peer_u[layer], peer_v[layer], ln2_g[layer], ln2_b[layer])
    return xs.reshape(b, s, d)
```

```python
import functools
import math
from typing import Any, NamedTuple

import jax
import jax.numpy as jnp
import numpy as np
from jax import lax
from jax.experimental import pallas as pl
from jax.experimental.pallas import tpu as pltpu

D_MODEL = 4096
SEQ = 8192
DEPTH = 1
HEAD_DIM = 128
MOBA_HEADS = 16
MOBA_WIDTH = MOBA_HEADS * HEAD_DIM
MOBA_BLOCK = 256
MOBA_TOPK = 3
N_MOBA_BLOCKS = SEQ // MOBA_BLOCK
DIFF_HEADS = 8
DIFF_V_DIM = 2 * HEAD_DIM
DIFF_WIDTH = DIFF_HEADS * DIFF_V_DIM
N_BUCKETS = 32
MAX_DISTANCE = 128
PEER_HEADS = 8
PEER_N_KEYS = 128
PEER_HALF = 128
PEER_QUERY_DIM = 2 * PEER_HALF
PEER_TOPK = 16
DEEPNORM_ALPHA = (2.0 * DEPTH) ** 0.25
LN_EPS = 1e-5
NEG_INF = -1e30

LOG2E = math.log2(math.e)
QK_SCALE_LOG2 = HEAD_DIM ** -0.5 * LOG2E

LANES = 128
MXU_DIM = 256
VMEM_LIMIT_CAP = 60000 * 1024
COMPILER_SCRATCH_BYTES = 8 << 20

ATTN_Q_TILE = 1024
ATTN_K_TILE = 512
BIAS_TILE = LANES
ATTN_ROW_CHUNK = 32

_NT = (((1,), (1,)), ((), ()))
_TN = (((0,), (0,)), ((), ()))


def _nbytes(shape, dtype):
    return int(np.prod(shape)) * jnp.dtype(dtype).itemsize


def _params(semantics, pipelined=(), resident=()):
    need = 2 * sum(pipelined) + sum(resident) + COMPILER_SCRATCH_BYTES
    return pltpu.CompilerParams(dimension_semantics=semantics,
                                vmem_limit_bytes=min(need, VMEM_LIMIT_CAP))


_NN = (((1,), (0,)), ((), ()))


def _matmul_kernel(a_ref, b_ref, *rest):
    *scale, o_ref = rest
    acc = lax.dot_general(a_ref[...], b_ref[...], _NN, preferred_element_type=jnp.float32)
    for s_ref in scale:
        acc = acc * s_ref[...]
    o_ref[...] = acc.astype(o_ref.dtype)


def _matmul(a, w, col0, n, out_dtype, tm, tn, name, col_scale=None):
    m, k = a.shape
    c0 = col0 // tn
    scale_specs = [] if col_scale is None else [pl.BlockSpec((1, tn), lambda j, i: (0, j))]
    scale_args = [] if col_scale is None else [col_scale]
    return pl.pallas_call(
        _matmul_kernel,
        grid=(n // tn, m // tm),
        in_specs=[pl.BlockSpec((tm, k), lambda j, i: (i, 0)),
                  pl.BlockSpec((k, tn), lambda j, i: (0, c0 + j))] + scale_specs,
        out_specs=pl.BlockSpec((tm, tn), lambda j, i: (i, j)),
        out_shape=jax.ShapeDtypeStruct((m, n), out_dtype),
        compiler_params=_params(("arbitrary", "arbitrary"),
                                (_nbytes((tm, k), a.dtype), _nbytes((k, tn), w.dtype),
                                 _nbytes((tm, tn), out_dtype))),
        name=name,
    )(a, w, *scale_args)


def _moba_route_kernel(q_ref, k_ref, qa_ref, ka_ref, kmean_ref, *, tq):
    i = pl.program_id(1)

    @pl.when(i == 0)
    def _():
        k3 = k_ref[...].reshape(N_MOBA_BLOCKS, MOBA_BLOCK, HEAD_DIM)
        kmean_ref[...] = jnp.zeros_like(kmean_ref)
        kmean_ref[0:N_MOBA_BLOCKS, :] = jnp.sum(k3, axis=1) * (1.0 / MOBA_BLOCK)

    q = q_ref[...]
    shift = int(math.log2(MOBA_BLOCK))
    gate = lax.dot_general(kmean_ref[0:N_MOBA_BLOCKS, :], q, _NT, precision=lax.Precision.HIGHEST,
                           preferred_element_type=jnp.float32)
    blk = lax.broadcasted_iota(jnp.int32, (N_MOBA_BLOCKS, tq), 0)
    own_t = jnp.right_shift(i * tq + lax.broadcasted_iota(jnp.int32, (N_MOBA_BLOCKS, tq), 1), shift)
    blk_f = blk.astype(jnp.float32)
    past = blk < own_t
    g = jnp.where(past, gate, NEG_INF)
    picked = jnp.zeros((N_MOBA_BLOCKS, tq), jnp.float32)
    for _ in range(MOBA_TOPK):
        mx = jnp.max(g, axis=0, keepdims=True)
        first = jnp.min(jnp.where(g == mx, blk_f, float(N_MOBA_BLOCKS)), axis=0, keepdims=True)
        hit = blk_f == first
        picked = jnp.where(hit, 1.0, picked)
        g = jnp.where(hit, -jnp.inf, g)
    allowed = jnp.where(past, picked, 0.0)
    allowed = jnp.where(blk == own_t, 1.0, allowed)
    blocked = jnp.concatenate([1.0 - allowed, jnp.zeros((LANES - N_MOBA_BLOCKS, tq), jnp.float32)],
                              axis=0).T
    lane = lax.broadcasted_iota(jnp.int32, (tq, LANES), 1)
    own = jnp.right_shift(i * tq + lax.broadcasted_iota(jnp.int32, (tq, LANES), 0), shift)
    qa_ref[:, :HEAD_DIM] = (q * QK_SCALE_LOG2).astype(qa_ref.dtype)
    qa_ref[:, HEAD_DIM:] = blocked.astype(qa_ref.dtype)
    kb = k_ref[pl.ds(pl.multiple_of(i * tq, tq), tq), :]
    ka_ref[:, :HEAD_DIM] = kb.astype(ka_ref.dtype)
    ka_ref[:, HEAD_DIM:] = jnp.where(lane == own, NEG_INF, 0.0).astype(ka_ref.dtype)


def _moba_route(mqk, tq=1024):
    s = mqk.shape[0]
    aug = 2 * HEAD_DIM
    out = jax.ShapeDtypeStruct((s, MOBA_HEADS * aug), jnp.bfloat16)
    return pl.pallas_call(
        functools.partial(_moba_route_kernel, tq=tq),
        grid=(MOBA_HEADS, s // tq),
        in_specs=[pl.BlockSpec((tq, HEAD_DIM), lambda h, i: (i, h)),
                  pl.BlockSpec((s, HEAD_DIM), lambda h, i: (0, MOBA_HEADS + h))],
        out_specs=[pl.BlockSpec((tq, aug), lambda h, i: (i, h)),
                   pl.BlockSpec((tq, aug), lambda h, i: (i, h))],
        out_shape=[out, out],
        scratch_shapes=[pltpu.VMEM((LANES, HEAD_DIM), jnp.float32)],
        compiler_params=_params(("arbitrary", "arbitrary"),
                                (_nbytes((tq, HEAD_DIM), jnp.float32),
                                 _nbytes((s, HEAD_DIM), jnp.float32),
                                 2 * _nbytes((tq, aug), jnp.bfloat16))),
        name="moba_route",
    )(mqk, mqk)


def _t5_bucket(dist, xp):
    n = xp.maximum(dist, 0)
    max_exact = N_BUCKETS // 2
    nf = xp.maximum(n, 1).astype(xp.float32)
    large = max_exact + (xp.log(nf / max_exact) / math.log(MAX_DISTANCE / max_exact)
                         * (N_BUCKETS - max_exact)).astype(xp.int32)
    large = xp.minimum(large, N_BUCKETS - 1)
    return xp.where(n < max_exact, n, large)


def _toeplitz(g, t):
    heads = g.shape[0]
    rows = jnp.tile(g, (1, t))[:, :t * (2 * t - 1)].reshape(heads, t, 2 * t - 1)
    return rows[:, :, :t]


def _bias_tiles(rel_bias, t):
    assert int(_t5_bucket(np.array([t + 1]), np)[0]) == N_BUCKETS - 1
    tab = rel_bias.T.astype(jnp.float32)
    rel = (tab - tab[:, N_BUCKETS - 1:]) * LOG2E
    c = jnp.arange(2 * t, dtype=jnp.int32)
    d0 = jnp.where(c == 0, 0, 2 * t - c)
    g0 = jnp.where((c == 0) | (c > t), rel[:, _t5_bucket(d0, jnp)], NEG_INF)
    d1 = jnp.where(c < t, t - c, 3 * t - c)
    g1 = rel[:, _t5_bucket(d1, jnp)]
    return _toeplitz(g0, t), _toeplitz(g1, t)


class _Stream(NamedTuple):
    q_ref: Any
    k_ref: Any
    v_ref: Any
    b0_ref: Any
    b1_ref: Any
    s_ref: Any
    p_ref: Any
    m_ref: Any
    a_ref: Any
    l_ref: Any
    acc_ref: Any


_STREAM_SCRATCH = ("s_ref", "p_ref", "m_ref", "a_ref", "l_ref", "acc_ref")


def _stream_scratch(tq, tk, dv):
    return [pltpu.VMEM((tq, tk), jnp.float32), pltpu.VMEM((tq, tk), jnp.bfloat16),
            pltpu.VMEM((tq, LANES), jnp.float32), pltpu.VMEM((tq, LANES), jnp.float32),
            pltpu.VMEM((tq, LANES), jnp.float32), pltpu.VMEM((tq, dv), jnp.float32)]


def _stream_scratch_bytes(tq, tk, dv):
    return (_nbytes((tq, tk), jnp.float32) + _nbytes((tq, tk), jnp.bfloat16)
            + 3 * _nbytes((tq, LANES), jnp.float32) + _nbytes((tq, dv), jnp.float32))


def _attn_step(streams, j, lead, tq, tk):
    off = pl.multiple_of(j * tk, tk)
    n_blocks = tk // BIAS_TILE
    row0 = 0 if lead is None else max(0, -lead) * BIAS_TILE
    for st in streams:
        st.s_ref[row0:, :] = lax.dot_general(st.q_ref[row0:, :], st.k_ref[pl.ds(off, tk), :], _NT,
                                             preferred_element_type=jnp.float32)
    for st in streams:
        for r in range(row0 // ATTN_ROW_CHUNK, tq // ATTN_ROW_CHUNK):
            rows = slice(r * ATTN_ROW_CHUNK, (r + 1) * ATTN_ROW_CHUNK)
            qb, in_block = divmod(r * ATTN_ROW_CHUNK, BIAS_TILE)
            bias_rows = slice(in_block, in_block + ATTN_ROW_CHUNK)
            blocks = {}
            for c in range(n_blocks):
                gap = 2 if lead is None else qb + lead - c
                if gap < 0:
                    continue
                blk = st.s_ref[rows, c * BIAS_TILE:(c + 1) * BIAS_TILE]
                if gap == 0:
                    blk = blk + st.b0_ref[bias_rows, :]
                elif gap == 1:
                    blk = blk + st.b1_ref[bias_rows, :]
                blocks[c] = blk
            mx = functools.reduce(jnp.maximum, blocks.values())
            m_prev = st.m_ref[rows, :]
            m_new = jnp.maximum(m_prev, jnp.max(mx, axis=1, keepdims=True))
            alpha = jnp.exp2(m_prev - m_new)
            probs = {c: jnp.exp2(blk - m_new) for c, blk in blocks.items()}
            st.m_ref[rows, :] = m_new
            st.a_ref[rows, :] = alpha
            st.l_ref[rows, :] = alpha * st.l_ref[rows, :] + functools.reduce(jnp.add, probs.values())
            for c in range(n_blocks):
                p = probs[c].astype(st.p_ref.dtype) if c in probs else jnp.zeros(
                    (ATTN_ROW_CHUNK, BIAS_TILE), st.p_ref.dtype)
                st.p_ref[rows, c * BIAS_TILE:(c + 1) * BIAS_TILE] = p
    for st in streams:
        pv = jnp.dot(st.p_ref[row0:, :], st.v_ref[pl.ds(off, tk), :], preferred_element_type=jnp.float32)
        for c in range(pv.shape[1] // LANES):
            cols = slice(c * LANES, (c + 1) * LANES)
            st.acc_ref[row0:, cols] = st.a_ref[row0:, :] * st.acc_ref[row0:, cols] + pv[:, cols]


def _alt_streams(streams, scratch):
    return [st._replace(s_ref=scratch[3 * n], p_ref=scratch[3 * n + 1], a_ref=scratch[3 * n + 2])
            for n, st in enumerate(streams)]


def _alt_scratch(tq, tk):
    return [pltpu.VMEM((tq, tk), jnp.float32), pltpu.VMEM((tq, tk), jnp.bfloat16),
            pltpu.VMEM((tq, LANES), jnp.float32)]


def _attend(streams, alt_streams, i, tq, tk):
    for st in streams:
        st.m_ref[...] = jnp.full_like(st.m_ref, -jnp.inf)
        st.l_ref[...] = jnp.zeros_like(st.l_ref)
        st.acc_ref[...] = jnp.zeros_like(st.acc_ref)

    ratio = tq // tk
    first = ratio * i
    blocks_per_tile = tk // BIAS_TILE

    def far(j, carry):
        _attn_step(streams, j, None, tq, tk)
        return carry

    lax.fori_loop(0, jnp.maximum(first - 1, 0), far, 0)

    @pl.when(i > 0)
    def _():
        _attn_step(streams, first - 1, blocks_per_tile, tq, tk)

    for d in range(ratio):
        _attn_step(streams if d % 2 == 0 else alt_streams, first + d, -d * blocks_per_tile, tq, tk)
    return [st.acc_ref[...] / jnp.sum(st.l_ref[...], axis=1, keepdims=True) for st in streams]


def _moba_attn_kernel(qa_ref, qb_ref, ka_ref, kb_ref, va_ref, vb_ref, b0a_ref, b0b_ref, b1a_ref, b1b_ref,
                      o_ref, *scratch, tq, tk):
    n = len(_STREAM_SCRATCH)
    streams = [_Stream(qa_ref, ka_ref, va_ref, b0a_ref, b1a_ref, *scratch[:n]),
               _Stream(qb_ref, kb_ref, vb_ref, b0b_ref, b1b_ref, *scratch[n:2 * n])]
    oa, ob = _attend(streams, _alt_streams(streams, scratch[2 * n:]), pl.program_id(1), tq, tk)
    o_ref[:, :HEAD_DIM] = oa.astype(o_ref.dtype)
    o_ref[:, HEAD_DIM:] = ob.astype(o_ref.dtype)


def _moba_attention(qa, ka, v, v_col0, b0, b1, tq=ATTN_Q_TILE, tk=ATTN_K_TILE):
    s = qa.shape[0]
    aug = 2 * HEAD_DIM

    def pair(spec_of_head):
        return [spec_of_head(0), spec_of_head(1)]

    return pl.pallas_call(
        functools.partial(_moba_attn_kernel, tq=tq, tk=tk),
        grid=(MOBA_HEADS // 2, s // tq),
        in_specs=(pair(lambda d: pl.BlockSpec((tq, aug), lambda g, i: (i, 2 * g + d)))
                  + pair(lambda d: pl.BlockSpec((s, aug), lambda g, i: (0, 2 * g + d)))
                  + pair(lambda d: pl.BlockSpec((s, HEAD_DIM), lambda g, i: (0, v_col0 + 2 * g + d)))
                  + pair(lambda d: pl.BlockSpec((None, BIAS_TILE, BIAS_TILE), lambda g, i: (2 * g + d, 0, 0)))
                  + pair(lambda d: pl.BlockSpec((None, BIAS_TILE, BIAS_TILE), lambda g, i: (2 * g + d, 0, 0)))),
        out_specs=pl.BlockSpec((tq, 2 * HEAD_DIM), lambda g, i: (i, g)),
        out_shape=jax.ShapeDtypeStruct((s, MOBA_WIDTH), jnp.bfloat16),
        scratch_shapes=2 * _stream_scratch(tq, tk, HEAD_DIM) + 2 * _alt_scratch(tq, tk),
        compiler_params=_params(("arbitrary", "arbitrary"),
                                (2 * _nbytes((tq, aug), jnp.bfloat16), 2 * _nbytes((s, aug), jnp.bfloat16),
                                 2 * _nbytes((s, HEAD_DIM), jnp.bfloat16),
                                 4 * _nbytes((BIAS_TILE, BIAS_TILE), jnp.float32),
                                 _nbytes((tq, 2 * HEAD_DIM), jnp.bfloat16)),
                                (4 * _stream_scratch_bytes(tq, tk, HEAD_DIM),)),
        name="moba_attention",
    )(qa, qa, ka, ka, v, v, b0, b0, b1, b1)


def _diff_attn_kernel(lam_ref, q1_ref, q2_ref, k1_ref, k2_ref, v_ref, b0_ref, b1_ref, g_ref, o_ref,
                      *scratch, tq, tk, out_scale):
    n = len(_STREAM_SCRATCH)
    streams = [_Stream(q1_ref, k1_ref, v_ref, b0_ref, b1_ref, *scratch[:n]),
               _Stream(q2_ref, k2_ref, v_ref, b0_ref, b1_ref, *scratch[n:2 * n])]
    o1, o2 = _attend(streams, _alt_streams(streams, scratch[2 * n:]), pl.program_id(1), tq, tk)
    y = o1 - lam_ref[0, 0] * o2
    y = y * lax.rsqrt(jnp.mean(y * y, axis=1, keepdims=True) + LN_EPS) * g_ref[...] * out_scale
    o_ref[...] = y.astype(o_ref.dtype)


def _diff_attention(lam, proj, q_col0, k_col0, v_col0, b0, b1, subln_g, out_scale,
                    tq=ATTN_Q_TILE, tk=ATTN_K_TILE):
    s = proj.shape[0]
    return pl.pallas_call(
        functools.partial(_diff_attn_kernel, tq=tq, tk=tk, out_scale=out_scale),
        grid=(DIFF_HEADS, s // tq),
        in_specs=[pl.BlockSpec(memory_space=pltpu.SMEM),
                  pl.BlockSpec((tq, HEAD_DIM), lambda h, i: (i, q_col0 + 2 * h)),
                  pl.BlockSpec((tq, HEAD_DIM), lambda h, i: (i, q_col0 + 2 * h + 1)),
                  pl.BlockSpec((s, HEAD_DIM), lambda h, i: (0, k_col0 + 2 * h)),
                  pl.BlockSpec((s, HEAD_DIM), lambda h, i: (0, k_col0 + 2 * h + 1)),
                  pl.BlockSpec((s, DIFF_V_DIM), lambda h, i: (0, v_col0 + h)),
                  pl.BlockSpec((None, BIAS_TILE, BIAS_TILE), lambda h, i: (MOBA_HEADS + h, 0, 0)),
                  pl.BlockSpec((None, BIAS_TILE, BIAS_TILE), lambda h, i: (MOBA_HEADS + h, 0, 0)),
                  pl.BlockSpec((1, DIFF_V_DIM), lambda h, i: (0, 0))],
        out_specs=pl.BlockSpec((tq, DIFF_V_DIM), lambda h, i: (i, h)),
        out_shape=jax.ShapeDtypeStruct((s, DIFF_WIDTH), jnp.bfloat16),
        scratch_shapes=2 * _stream_scratch(tq, tk, DIFF_V_DIM) + 2 * _alt_scratch(tq, tk),
        compiler_params=_params(("arbitrary", "arbitrary"),
                                (2 * _nbytes((tq, HEAD_DIM), jnp.bfloat16),
                                 2 * _nbytes((s, HEAD_DIM), jnp.bfloat16),
                                 _nbytes((s, DIFF_V_DIM), jnp.bfloat16),
                                 2 * _nbytes((BIAS_TILE, BIAS_TILE), jnp.float32),
                                 _nbytes((tq, DIFF_V_DIM), jnp.bfloat16)),
                                (4 * _stream_scratch_bytes(tq, tk, DIFF_V_DIM),)),
        name="diff_attention",
    )(lam, proj, proj, proj, proj, proj, b0, b1, subln_g)


def _sigmoid(x):
    return 1.0 / (1.0 + jnp.exp(-x))


def _merge_kernel(ya_ref, yb_ref, wa_ref, wb_ref, ga_ref, gb_ref, ba_ref, bb_ref, o_ref):
    pa = lax.dot_general(ya_ref[...], wa_ref[...], _NN, preferred_element_type=jnp.float32)
    pb = lax.dot_general(yb_ref[...], wb_ref[...], _NN, preferred_element_type=jnp.float32)
    merged = (_sigmoid(ga_ref[...] + ba_ref[...]) * pa + _sigmoid(gb_ref[...] + bb_ref[...]) * pb)
    o_ref[...] = merged.astype(o_ref.dtype)


def _gated_merge(ya, yb, wa, wb, gates, gate_b, tm=512, tn=1024):
    s, k = ya.shape
    n = wa.shape[1]
    nb = n // tn
    return pl.pallas_call(
        _merge_kernel,
        grid=(nb, s // tm),
        in_specs=[pl.BlockSpec((tm, k), lambda j, i: (i, 0)),
                  pl.BlockSpec((tm, k), lambda j, i: (i, 0)),
                  pl.BlockSpec((k, tn), lambda j, i: (0, j)),
                  pl.BlockSpec((k, tn), lambda j, i: (0, j)),
                  pl.BlockSpec((tm, tn), lambda j, i: (i, j)),
                  pl.BlockSpec((tm, tn), lambda j, i: (i, nb + j)),
                  pl.BlockSpec((1, tn), lambda j, i: (0, j)),
                  pl.BlockSpec((1, tn), lambda j, i: (0, nb + j))],
        out_specs=pl.BlockSpec((tm, tn), lambda j, i: (i, j)),
        out_shape=jax.ShapeDtypeStruct((s, n), jnp.bfloat16),
        compiler_params=_params(("arbitrary", "arbitrary"),
                                (2 * _nbytes((tm, k), ya.dtype), 2 * _nbytes((k, tn), wa.dtype),
                                 2 * _nbytes((tm, tn), gates.dtype),
                                 _nbytes((tm, tn), jnp.bfloat16))),
        name="gated_merge",
    )(ya, yb, wa, wb, gates, gates, gate_b, gate_b)


def _residual_ln_kernel(x_ref, h_ref, g_ref, b_ref, *o_refs):
    z = DEEPNORM_ALPHA * x_ref[...] + h_ref[...]
    mu = jnp.mean(z, axis=1, keepdims=True)
    zc = z - mu
    var = jnp.mean(zc * zc, axis=1, keepdims=True)
    y = zc * lax.rsqrt(var + LN_EPS) * g_ref[...] + b_ref[...]
    for o_ref in o_refs:
        o_ref[...] = y.astype(o_ref.dtype)


def _residual_ln(x, h, g, b, out_dtypes, tm=256):
    s, d = x.shape
    row = pl.BlockSpec((tm, d), lambda i: (i, 0))
    vec = pl.BlockSpec((1, d), lambda i: (0, 0))
    return pl.pallas_call(
        _residual_ln_kernel,
        grid=(s // tm,),
        in_specs=[row, row, vec, vec],
        out_specs=[row for _ in out_dtypes],
        out_shape=[jax.ShapeDtypeStruct((s, d), dt) for dt in out_dtypes],
        compiler_params=_params(("arbitrary",),
                                [_nbytes((tm, d), x.dtype), _nbytes((tm, d), h.dtype)]
                                + [_nbytes((tm, d), dt) for dt in out_dtypes]),
        name="residual_layernorm",
    )(x, h, g, b)


def _top_rows_one_by_one(s, k):
    n = s.shape[0]
    ridx = lax.broadcasted_iota(jnp.int32, s.shape, 0).astype(jnp.float32)
    rows, rank = [], jnp.full(s.shape, float(k), jnp.float32)
    for i in range(k):
        mx = jnp.max(s, axis=0, keepdims=True)
        rows.append(mx)
        first = jnp.min(jnp.where(s == mx, ridx, float(n)), axis=0, keepdims=True)
        hit = ridx == first
        rank = jnp.where(hit, float(i), rank)
        s = jnp.where(hit, -jnp.inf, s)
    return tuple(rows) + (rank,)


def _top_rows(s, k):
    rows, rest, rank = [], s, jnp.full(s.shape, float(k), jnp.float32)
    for i in range(k):
        mx = jnp.max(rest, axis=0, keepdims=True)
        rows.append(mx)
        hit = rest == mx
        rank = jnp.where(hit, float(i), rank)
        rest = jnp.where(hit, -jnp.inf, rest)
    removed = jnp.sum(jnp.where(rest == -jnp.inf, 1.0, 0.0), axis=0, keepdims=True)
    out = lax.cond(jnp.max(removed) > k, lambda: _top_rows_one_by_one(s, k),
                   lambda: tuple(rows) + (rank,))
    return out[:k], out[k]


def _stack_rows(rows):
    k, tm = len(rows), rows[0].shape[1]
    ridx = lax.broadcasted_iota(jnp.int32, (k, tm), 0)
    out = jnp.zeros((k, tm), rows[0].dtype)
    for r, row in enumerate(rows):
        out = jnp.where(ridx == r, row, out)
    return out


def _peer_route_kernel(q_ref, k1_ref, k2_ref, c1_ref, e1_ref, r2_ref, e2_ref, *, tm):
    q = q_ref[...]
    s1 = lax.dot_general(k1_ref[...], q[:, :PEER_HALF], _NT, precision=lax.Precision.HIGHEST,
                         preferred_element_type=jnp.float32)
    s2 = lax.dot_general(k2_ref[...], q[:, PEER_HALF:], _NT, precision=lax.Precision.HIGHEST,
                         preferred_element_type=jnp.float32)
    v1, rank1 = _top_rows(s1, PEER_TOPK)
    v2, rank2 = _top_rows(s2, PEER_TOPK)
    v1m = _stack_rows(v1)
    v2m = _stack_rows(v2)
    half = PEER_TOPK // 2
    assert all((a + 1) * (b + 1) > PEER_TOPK for a in range(1, half) for b in range(half, PEER_TOPK))
    assert all((a + 1) * (b + 1) > PEER_TOPK for a in range(half, PEER_TOPK) for b in range(1, PEER_TOPK))
    cand = jnp.concatenate([v1[0] + v2m] + [v1[a] + v2m[:half] for a in range(1, half)]
                           + [v1m[half:] + v2[0]], axis=0)
    top, _ = _top_rows(cand, PEER_TOPK)
    thr = top[PEER_TOPK - 1]
    sums = [v1[a] + v2m for a in range(PEER_TOPK)]
    above = [jnp.sum(jnp.where(sm > thr, 1.0, 0.0), axis=0, keepdims=True) for sm in sums]
    equal = [jnp.sum(jnp.where(sm == thr, 1.0, 0.0), axis=0, keepdims=True) for sm in sums]
    spare = float(PEER_TOPK) - functools.reduce(jnp.add, above)
    bidx = lax.broadcasted_iota(jnp.int32, (PEER_TOPK, tm), 0).astype(jnp.float32)
    c1 = jnp.zeros_like(s1)
    z = jnp.zeros_like(thr)
    for a in range(PEER_TOPK):
        taken = jnp.minimum(equal[a], spare)
        spare = spare - taken
        count = above[a] + taken
        z = z + jnp.sum(jnp.where(bidx < count, jnp.exp(sums[a] - top[0]), 0.0), axis=0, keepdims=True)
        c1 = jnp.where(rank1 == float(a), count, c1)
    c1_ref[...] = c1
    e1_ref[...] = jnp.exp(s1 - v1[0])
    r2_ref[...] = rank2.astype(r2_ref.dtype)
    e2_ref[...] = (jnp.exp(s2 - v2[0]) / z).astype(e2_ref.dtype)


def _peer_route(qp, keys1, keys2, tm=512):
    s = qp.shape[0]
    shape = (PEER_HEADS, PEER_N_KEYS, s)
    spec = pl.BlockSpec((None, PEER_N_KEYS, tm), lambda i, h: (h, 0, i))
    key_spec = pl.BlockSpec((PEER_N_KEYS, PEER_HALF), lambda i, h: (0, 0))
    return pl.pallas_call(
        functools.partial(_peer_route_kernel, tm=tm),
        grid=(s // tm, PEER_HEADS),
        in_specs=[pl.BlockSpec((tm, PEER_QUERY_DIM), lambda i, h: (i, h)), key_spec, key_spec],
        out_specs=[spec, spec, spec, spec],
        out_shape=[jax.ShapeDtypeStruct(shape, jnp.float32), jax.ShapeDtypeStruct(shape, jnp.float32),
                   jax.ShapeDtypeStruct(shape, jnp.bfloat16), jax.ShapeDtypeStruct(shape, jnp.bfloat16)],
        compiler_params=_params(("arbitrary", "arbitrary"),
                                (_nbytes((tm, PEER_QUERY_DIM), jnp.float32),
                                 2 * _nbytes((PEER_N_KEYS, PEER_HALF), jnp.float32),
                                 3 * _nbytes((PEER_N_KEYS, tm), jnp.float32))),
        name="peer_route",
    )(qp, keys1, keys2)


def _gelu(x):
    return 0.5 * x * (1.0 + lax.erf(x * math.sqrt(0.5)))


def _peer_expert_kernel(x_ref, u_ref, v_ref, c1_ref, e1_ref, r2_ref, e2_ref, o_ref, *w_refs, tm, te):
    e = pl.program_id(1)

    @pl.when(e == 0)
    def _():
        o_ref[...] = jnp.zeros_like(o_ref)

    gate_dtype = r2_ref.dtype
    groups = MXU_DIM // PEER_N_KEYS
    outs = []
    for chain, w_ref in enumerate(w_refs):
        base = chain * MXU_DIM
        act = lax.dot_general(u_ref[base:base + MXU_DIM, :], x_ref[...], _NT,
                              preferred_element_type=jnp.float32)
        act = _gelu(act).astype(gate_dtype)
        for c in range(groups):
            i1 = e * (te // PEER_N_KEYS) + chain * groups + c
            g = None
            for h in range(PEER_HEADS):
                c1_row = c1_ref[h, pl.ds(i1, 1), :].astype(gate_dtype)
                e1_row = e1_ref[h, pl.ds(i1, 1), :].astype(gate_dtype)
                term = jnp.where(r2_ref[h] < c1_row, e2_ref[h], 0.0) * e1_row
                g = term if g is None else g + term
            lo = c * PEER_N_KEYS
            w_ref[lo:lo + PEER_N_KEYS, :] = g * act[lo:lo + PEER_N_KEYS, :]
        outs.append(lax.dot_general(w_ref[...], v_ref[base:base + MXU_DIM, :], _TN,
                                    preferred_element_type=jnp.float32))
    o_ref[...] += functools.reduce(jnp.add, outs)


def _peer_experts(x, u, v, c1, e1, r2, e2, tm=512, te=512):
    s, d = x.shape
    n_exp = u.shape[0]
    once = pl.Buffered(1)
    gate_spec = pl.BlockSpec((PEER_HEADS, PEER_N_KEYS, tm), lambda i, e: (0, 0, i), pipeline_mode=once)
    gate_bytes = sum(_nbytes((PEER_HEADS, PEER_N_KEYS, tm), a.dtype) for a in (c1, e1, r2, e2))
    return pl.pallas_call(
        functools.partial(_peer_expert_kernel, tm=tm, te=te),
        grid=(s // tm, n_exp // te),
        in_specs=[pl.BlockSpec((tm, d), lambda i, e: (i, 0), pipeline_mode=once),
                  pl.BlockSpec((te, d), lambda i, e: (e, 0)),
                  pl.BlockSpec((te, d), lambda i, e: (e, 0)),
                  gate_spec, gate_spec, gate_spec, gate_spec],
        out_specs=pl.BlockSpec((tm, d), lambda i, e: (i, 0)),
        out_shape=jax.ShapeDtypeStruct((s, d), jnp.float32),
        scratch_shapes=[pltpu.VMEM((MXU_DIM, tm), r2.dtype) for _ in range(te // MXU_DIM)],
        compiler_params=_params(("arbitrary", "arbitrary"),
                                (2 * _nbytes((te, d), u.dtype), _nbytes((tm, d), jnp.float32)),
                                (_nbytes((tm, d), x.dtype), gate_bytes, _nbytes((te, tm), r2.dtype))),
        name="peer_experts",
    )(x, u, v, c1, e1, r2, e2)


def _layer(x, layer, w_in, gate_b, rel_bias, lq1, lk1, lq2, lk2, subln_g, w_o_moba, w_o_diff, w_out,
           ln1_g, ln1_b, peer_wq, keys1, keys2, peer_u, peer_v, ln2_g, ln2_b):
    bf16 = jnp.bfloat16
    f32 = jnp.float32
    lambda_init = 0.8 - 0.6 * math.exp(-0.3 * layer)
    xb = x.astype(bf16)

    c0 = 2 * MOBA_WIDTH
    c1 = c0 + MOBA_WIDTH + 3 * DIFF_WIDTH
    col_scale = jnp.concatenate([jnp.ones((MOBA_WIDTH,), f32), jnp.full((DIFF_WIDTH,), QK_SCALE_LOG2, f32),
                                 jnp.ones((2 * DIFF_WIDTH,), f32)]).reshape(1, c1 - c0)
    mqk = _matmul(xb, w_in, 0, c0, f32, 512, 1024, "proj_moba_qk")
    mid = _matmul(xb, w_in, c0, c1 - c0, bf16, 512, 1024, "proj_attn", col_scale)
    gates = _matmul(xb, w_in, c1, 2 * D_MODEL, bf16, 512, 1024, "proj_gates")

    b0, b1 = _bias_tiles(rel_bias, BIAS_TILE)
    qa, ka = _moba_route(mqk)
    y_a = _moba_attention(qa, ka, mid, 0, b0, b1)

    lam = (jnp.exp(jnp.sum(lq1.astype(f32) * lk1.astype(f32)))
           - jnp.exp(jnp.sum(lq2.astype(f32) * lk2.astype(f32))) + lambda_init).reshape(1, 1)
    hb = MOBA_WIDTH // HEAD_DIM
    y_b = _diff_attention(lam, mid, hb, hb + DIFF_WIDTH // HEAD_DIM,
                          (MOBA_WIDTH + 2 * DIFF_WIDTH) // DIFF_V_DIM, b0, b1,
                          subln_g.reshape(1, DIFF_V_DIM).astype(f32), 1.0 - lambda_init)

    merged = _gated_merge(y_a, y_b, w_o_moba, w_o_diff, gates,
                          gate_b.reshape(1, 2 * D_MODEL).astype(f32))
    h = _matmul(merged, w_out, 0, D_MODEL, bf16, 512, 1024, "proj_out")
    x1, x1b = _residual_ln(x, h, ln1_g.reshape(1, D_MODEL), ln1_b.reshape(1, D_MODEL), (f32, bf16))

    qp = _matmul(x1b, peer_wq, 0, PEER_HEADS * PEER_QUERY_DIM, f32, 512, 1024, "peer_query")
    c1, e1, r2, e2 = _peer_route(qp, keys1, keys2)
    h2 = _peer_experts(x1b, peer_u.astype(bf16), peer_v.astype(bf16), c1, e1, r2, e2)
    (x2,) = _residual_ln(x1, h2, ln2_g.reshape(1, D_MODEL), ln2_b.reshape(1, D_MODEL), (f32,))
    return x2


def kernel(x, w_in, gate_b, rel_bias, lambda_q1, lambda_k1, lambda_q2, lambda_k2, subln_g, w_o_moba,
           w_o_diff, w_out, ln1_g, ln1_b, peer_wq, peer_keys1, peer_keys2, peer_u, peer_v, ln2_g, ln2_b):
    b, s, d = x.shape
    assert (b, s, d) == (1, SEQ, D_MODEL) and w_in.shape[0] == DEPTH
    xs = x.reshape(s, d)
    for layer in range(DEPTH):
        xs = _layer(xs, layer, w_in[layer], gate_b[layer], rel_bias, lambda_q1[layer], lambda_k1[layer],
                    lambda_q2[layer], lambda_k2[layer], subln_g[layer], w_o_moba[layer], w_o_diff[layer],
                    w_out[layer], ln1_g[layer], ln1_b[layer], peer_wq[layer], peer_keys1[layer],
                    peer_keys2[layer], peer_u[layer], peer_v[layer], ln2_g[layer], ln2_b[layer])
    return xs.reshape(b, s, d)
```
